```python
import math
import jax, jax.numpy as jnp
from jax import lax
import numpy as np

D_MODEL = 2048
BATCH = 4
SEQ = 4096
DEPTH = 2

MEM_LEN = 256
CHUNK = 128
GMLP_WIDTH = 1024
GMLP_GROUPS = 8
GMLP_GROUP_DIM = GMLP_WIDTH // GMLP_GROUPS
N_Q_HEADS = 16
N_KV_HEADS = 4
HEAD_DIM = 64
ATTN_WIDTH = N_Q_HEADS * HEAD_DIM
KV_WIDTH = N_KV_HEADS * HEAD_DIM
WINDOW = 128
ROPE_THETA = 10000.0
X_HEADS = 4
X_HEAD_DIM = 128
X_WIDTH = X_HEADS * X_HEAD_DIM
D_FF = 4 * D_MODEL
LN_EPS = 1e-5
ALPHA = (2 * DEPTH) ** 0.25
BETA = (8 * DEPTH) ** -0.25

OFF_U = GMLP_WIDTH
OFF_V = 2 * GMLP_WIDTH
OFF_Q = OFF_V + ATTN_WIDTH
OFF_K = OFF_Q + KV_WIDTH
OFF_VA = OFF_K + KV_WIDTH
OFF_GA = OFF_VA + D_MODEL
IN_WIDTH = OFF_GA + D_MODEL

kernel_name = "hybrid_gmlp_swa_sink_deepnorm_decoder"


def layer_norm(x, g, b):
    xf = x.astype(jnp.float32)
    mu = jnp.mean(xf, axis=-1, keepdims=True)
    var = jnp.mean(jnp.square(xf - mu), axis=-1, keepdims=True)
    y = (xf - mu) * lax.rsqrt(var + LN_EPS)
    return (y * g.astype(jnp.float32) + b.astype(jnp.float32)).astype(x.dtype)


def rope_tables(seq):
    inv = 1.0 / (ROPE_THETA ** (jnp.arange(0, HEAD_DIM, 2, dtype=jnp.float32) / HEAD_DIM))
    pos = jnp.arange(seq, dtype=jnp.float32)
    ang = pos[:, None] * inv[None, :]
    return jnp.cos(ang), jnp.sin(ang)


def apply_rope(x, cos, sin):
    xf = x.astype(jnp.float32)
    x1, x2 = jnp.split(xf, 2, axis=-1)
    c = cos[None, :, None, :]
    s = sin[None, :, None, :]
    return jnp.concatenate([x1 * c - x2 * s, x2 * c + x1 * s], axis=-1).astype(x.dtype)


def chunked_spatial_gating(u, v, ln_g, ln_b, w_s, b_s):
    bsz, seq, _ = u.shape
    n_chunks = seq // CHUNK
    v = layer_norm(v, ln_g, ln_b)
    vc = v.reshape(bsz, n_chunks, CHUNK, GMLP_GROUPS, GMLP_GROUP_DIM)
    causal = jnp.tril(jnp.ones((CHUNK, CHUNK), dtype=bool))
    w = jnp.where(causal[None], w_s, 0.0)
    mixed = jnp.einsum('gts,bnsgc->bntgc', w, vc) + b_s.T[None, None, :, :, None]
    return u * mixed.reshape(bsz, seq, GMLP_WIDTH)


def sliding_window_attention(q, k, v, sinks):
    bsz, seq, _, _ = q.shape
    n_blk = seq // WINDOW
    grp = N_Q_HEADS // N_KV_HEADS
    qb = q.reshape(bsz, n_blk, WINDOW, N_KV_HEADS, grp, HEAD_DIM)
    kb = k.reshape(bsz, n_blk, WINDOW, N_KV_HEADS, HEAD_DIM)
    vb = v.reshape(bsz, n_blk, WINDOW, N_KV_HEADS, HEAD_DIM)
    pad = ((0, 0), (1, 0), (0, 0), (0, 0), (0, 0))
    kk = jnp.concatenate([jnp.pad(kb, pad)[:, :-1], kb], axis=2)
    vv = jnp.concatenate([jnp.pad(vb, pad)[:, :-1], vb], axis=2)
    scores = jnp.einsum('bnqhgd,bnkhd->bnhgqk', qb, kk).astype(jnp.float32) * (HEAD_DIM ** -0.5)
    q_loc = jnp.arange(WINDOW)[:, None]
    k_loc = jnp.arange(2 * WINDOW)[None, :]
    band = (k_loc <= q_loc + WINDOW) & (k_loc > q_loc)
    blk = jnp.arange(n_blk)[:, None, None]
    valid = band[None] & (blk * WINDOW + k_loc[None] - WINDOW >= 0)
    scores = jnp.where(valid[None, :, None, None], scores, -jnp.inf)
    sink = sinks.astype(jnp.float32).reshape(N_KV_HEADS, grp)[None, None, :, :, None, None]
    m = jnp.maximum(jnp.max(scores, axis=-1, keepdims=True), sink)
    p = jnp.exp(scores - m)
    probs = (p / (jnp.sum(p, axis=-1, keepdims=True) + jnp.exp(sink - m))).astype(v.dtype)
    out = jnp.einsum('bnhgqk,bnkhd->bnqhgd', probs, vv)
    return out.reshape(bsz, seq, ATTN_WIDTH)


def hybrid_mixer(x, w_in, b_gate, ln_v_g, ln_v_b, w_s, b_s, sinks, w_br_a, w_br_b, w_o, cos, sin):
    bsz, seq, _ = x.shape
    proj = x @ w_in
    u, v, q, k, va, ga, gb = jnp.split(proj, [OFF_U, OFF_V, OFF_Q, OFF_K, OFF_VA, OFF_GA], axis=-1)
    ya = chunked_spatial_gating(jax.nn.gelu(u), jax.nn.gelu(v), ln_v_g, ln_v_b, w_s, b_s) @ w_br_a
    q = apply_rope(q.reshape(bsz, seq, N_Q_HEADS, HEAD_DIM), cos, sin)
    k = apply_rope(k.reshape(bsz, seq, N_KV_HEADS, HEAD_DIM), cos, sin)
    va = va.reshape(bsz, seq, N_KV_HEADS, HEAD_DIM)
    yb = sliding_window_attention(q, k, va, sinks) @ w_br_b
    merged = jax.nn.sigmoid(ga + b_gate[:D_MODEL]) * ya + jax.nn.sigmoid(gb + b_gate[D_MODEL:]) * yb
    return merged @ w_o


def memory_cross_attention(x, mem, w_xq, w_xkv, w_xo):
    bsz, seq, _ = x.shape
    q = (x @ w_xq).reshape(bsz, seq, X_HEADS, X_HEAD_DIM)
    k, v = jnp.split(mem @ w_xkv, 2, axis=-1)
    k = k.reshape(bsz, MEM_LEN, X_HEADS, X_HEAD_DIM)
    v = v.reshape(bsz, MEM_LEN, X_HEADS, X_HEAD_DIM)
    s = jnp.einsum('bqhd,bkhd->bhqk', q, k).astype(jnp.float32) * (X_HEAD_DIM ** -0.5)
    p = jax.nn.softmax(s, axis=-1).astype(v.dtype)
    o = jnp.einsum('bhqk,bkhd->bqhd', p, v).reshape(bsz, seq, X_WIDTH)
    return o @ w_xo


def squared_relu_mlp(x, w_up, w_down):
    return jnp.square(jax.nn.relu(x @ w_up)) @ w_down


def setup_inputs(seed: int = 0) -> dict:
    key = jax.random.key(seed)
    ks = jax.random.split(key, 26)
    f32 = jnp.float32
    L, D = DEPTH, D_MODEL

    def nrm(k, shape, scale):
        return jax.random.normal(k, shape, f32) * scale

    return {
        "x": nrm(ks[0], (BATCH, SEQ, D), 1.0),
        "mem": nrm(ks[1], (BATCH, MEM_LEN, D), 1.0),
        "w_in": nrm(ks[2], (L, D, IN_WIDTH), D ** -0.5),
        "b_gate": nrm(ks[3], (L, 2 * D), 0.1),
        "ln_v_g": 1.0 + nrm(ks[4], (L, GMLP_WIDTH), 0.02),
        "ln_v_b": nrm(ks[5], (L, GMLP_WIDTH), 0.02),
        "w_s": nrm(ks[6], (L, GMLP_GROUPS, CHUNK, CHUNK), 0.05),
        "b_s": 1.0 + nrm(ks[7], (L, GMLP_GROUPS, CHUNK), 0.1),
        "sinks": nrm(ks[8], (L, N_Q_HEADS), 0.5),
        "w_br_a": nrm(ks[9], (L, GMLP_WIDTH, D), GMLP_WIDTH ** -0.5),
        "w_br_b": nrm(ks[10], (L, ATTN_WIDTH, D), ATTN_WIDTH ** -0.5),
        "w_o": nrm(ks[11], (L, D, D), BETA * D ** -0.5),
        "ln1_g": 1.0 + nrm(ks[12], (L, D), 0.02),
        "ln1_b": nrm(ks[13], (L, D), 0.02),
        "w_xq": nrm(ks[14], (L, D, X_WIDTH), D ** -0.5),
        "w_xkv": nrm(ks[15], (L, D, 2 * X_WIDTH), D ** -0.5),
        "w_xo": nrm(ks[16], (L, X_WIDTH, D), BETA * X_WIDTH ** -0.5),
        "ln2_g": 1.0 + nrm(ks[17], (L, D), 0.02),
        "ln2_b": nrm(ks[18], (L, D), 0.02),
        "w_up": nrm(ks[19], (L, D, D_FF), D ** -0.5),
        "w_down": nrm(ks[20], (L, D_FF, D), BETA * D_FF ** -0.5),
        "ln3_g": 1.0 + nrm(ks[21], (L, D), 0.02),
        "ln3_b": nrm(ks[22], (L, D), 0.02),
    }


def reference(x, mem, w_in, b_gate, ln_v_g, ln_v_b, w_s, b_s, sinks, w_br_a, w_br_b, w_o,
              ln1_g, ln1_b, w_xq, w_xkv, w_xo, ln2_g, ln2_b, w_up, w_down, ln3_g, ln3_b):
    cos, sin = rope_tables(x.shape[1])
    for l in range(DEPTH):
        y = hybrid_mixer(x, w_in[l], b_gate[l], ln_v_g[l], ln_v_b[l], w_s[l], b_s[l], sinks[l],
                         w_br_a[l], w_br_b[l], w_o[l], cos, sin)
        x = layer_norm(ALPHA * x + y, ln1_g[l], ln1_b[l])
        y = memory_cross_attention(x, mem, w_xq[l], w_xkv[l], w_xo[l])
        x = layer_norm(ALPHA * x + y, ln2_g[l], ln2_b[l])
        y = squared_relu_mlp(x, w_up[l], w_down[l])
        x = layer_norm(ALPHA * x + y, ln3_g[l], ln3_b[l])
    return x
```

```python
import functools
import math

import jax
import jax.numpy as jnp
from jax import lax
from jax.experimental import pallas as pl
from jax.experimental.pallas import tpu as pltpu

D_MODEL = 2048
DEPTH = 2
MEM_LEN = 256
CHUNK = 128
GMLP_WIDTH = 1024
GMLP_GROUPS = 8
N_Q_HEADS = 16
N_KV_HEADS = 4
HEAD_DIM = 64
ATTN_WIDTH = N_Q_HEADS * HEAD_DIM
KV_WIDTH = N_KV_HEADS * HEAD_DIM
WINDOW = 128
ROPE_THETA = 10000.0
X_HEADS = 4
X_HEAD_DIM = 128
X_WIDTH = X_HEADS * X_HEAD_DIM
D_FF = 4 * D_MODEL
LN_EPS = 1e-5
ALPHA = (2 * DEPTH) ** 0.25

OFF_V = 2 * GMLP_WIDTH
OFF_VA = OFF_V + ATTN_WIDTH + 2 * KV_WIDTH
QKV_WIDTH = ATTN_WIDTH + 2 * KV_WIDTH

LANES = 128
VMEM_LIMIT_BYTES = 56 * 1024 * 1024

BF16 = jnp.bfloat16
F32 = jnp.float32


def _params(*semantics):
    return pltpu.CompilerParams(dimension_semantics=semantics, vmem_limit_bytes=VMEM_LIMIT_BYTES)


def _dot(a, b):
    return jnp.dot(a, b, preferred_element_type=F32)


def _dot_nt(a, b):
    return lax.dot_general(a, b, (((1,), (1,)), ((), ())), preferred_element_type=F32)


def _layer_norm(z, g, b):
    mu = jnp.mean(z, axis=-1, keepdims=True)
    zc = z - mu
    var = jnp.mean(zc * zc, axis=-1, keepdims=True)
    return zc * lax.rsqrt(var + LN_EPS) * g + b


def _gelu(x):
    return jax.nn.gelu(x, approximate=True)


def _proj_uv_kernel(x_ref, w_ref, g_ref, b_ref, gu_ref, vn_ref):
    acc = _dot(x_ref[...], w_ref[...])
    gu_ref[...] = _gelu(acc[:, :GMLP_WIDTH])
    v = _gelu(acc[:, GMLP_WIDTH:])
    vn_ref[...] = _layer_norm(v, g_ref[...], b_ref[...]).astype(BF16)


def _proj_uv(x_bf, w_uv, ln_g, ln_b, bm):
    t = x_bf.shape[0]
    return pl.pallas_call(
        _proj_uv_kernel,
        grid=(t // bm,),
        in_specs=[
            pl.BlockSpec((bm, D_MODEL), lambda i: (i, 0)),
            pl.BlockSpec((D_MODEL, 2 * GMLP_WIDTH), lambda i: (0, 0)),
            pl.BlockSpec((1, GMLP_WIDTH), lambda i: (0, 0)),
            pl.BlockSpec((1, GMLP_WIDTH), lambda i: (0, 0)),
        ],
        out_specs=[
            pl.BlockSpec((bm, GMLP_WIDTH), lambda i: (i, 0)),
            pl.BlockSpec((bm, GMLP_WIDTH), lambda i: (i, 0)),
        ],
        out_shape=[
            jax.ShapeDtypeStruct((t, GMLP_WIDTH), F32),
            jax.ShapeDtypeStruct((t, GMLP_WIDTH), BF16),
        ],
        compiler_params=_params("parallel"),
        name="proj_uv",
    )(x_bf, w_uv, ln_g, ln_b)


def _rope(x, cos, sin_signed, first_half):
    outs = []
    for c in range(x.shape[1] // LANES):
        xs = x[:, c * LANES:(c + 1) * LANES]
        swapped = jnp.where(first_half, pltpu.roll(xs, LANES - HEAD_DIM // 2, 1),
                            pltpu.roll(xs, HEAD_DIM // 2, 1))
        outs.append(xs * cos + swapped * sin_signed)
    return jnp.concatenate(outs, axis=1) if len(outs) > 1 else outs[0]


def _proj_qkv_kernel(x_ref, w_ref, cos_ref, sin_ref, q_ref, k_ref, v_ref):
    acc = _dot(x_ref[...], w_ref[...])
    cos = cos_ref[...]
    sin_signed = sin_ref[...]
    lane = lax.broadcasted_iota(jnp.int32, cos.shape, 1)
    first_half = (lane % HEAD_DIM) < (HEAD_DIM // 2)
    q_ref[...] = _rope(acc[:, :ATTN_WIDTH], cos, sin_signed, first_half).astype(BF16)
    k_ref[...] = _rope(acc[:, ATTN_WIDTH:ATTN_WIDTH + KV_WIDTH], cos, sin_signed, first_half).astype(BF16)
    v_ref[...] = acc[:, ATTN_WIDTH + KV_WIDTH:].astype(BF16)


def _proj_qkv(x_bf, w_qkv, cos_t, sin_t, bm, seq):
    t = x_bf.shape[0]
    n_seq_blocks = seq // bm
    return pl.pallas_call(
        _proj_qkv_kernel,
        grid=(t // bm,),
        in_specs=[
            pl.BlockSpec((bm, D_MODEL), lambda i: (i, 0)),
            pl.BlockSpec((D_MODEL, QKV_WIDTH), lambda i: (0, 0)),
            pl.BlockSpec((bm, LANES), lambda i: (i % n_seq_blocks, 0)),
            pl.BlockSpec((bm, LANES), lambda i: (i % n_seq_blocks, 0)),
        ],
        out_specs=[
            pl.BlockSpec((bm, ATTN_WIDTH), lambda i: (i, 0)),
            pl.BlockSpec((bm, KV_WIDTH), lambda i: (i, 0)),
            pl.BlockSpec((bm, KV_WIDTH), lambda i: (i, 0)),
        ],
        out_shape=[
            jax.ShapeDtypeStruct((t, ATTN_WIDTH), BF16),
            jax.ShapeDtypeStruct((t, KV_WIDTH), BF16),
            jax.ShapeDtypeStruct((t, KV_WIDTH), BF16),
        ],
        compiler_params=_params("parallel"),
        name="proj_qkv",
    )(x_bf, w_qkv, cos_t, sin_t)


def _proj_gate_kernel(x_ref, w_ref, b_ref, o_ref):
    acc = _dot(x_ref[...], w_ref[...])
    o_ref[...] = jax.nn.sigmoid(acc + b_ref[...])


def _proj_gate(x_bf, w_g, b_gate, bm, bn):
    t = x_bf.shape[0]
    n = w_g.shape[1]
    return pl.pallas_call(
        _proj_gate_kernel,
        grid=(t // bm, n // bn),
        in_specs=[
            pl.BlockSpec((bm, D_MODEL), lambda i, j: (i, 0)),
            pl.BlockSpec((D_MODEL, bn), lambda i, j: (0, j)),
            pl.BlockSpec((1, bn), lambda i, j: (0, j)),
        ],
        out_specs=pl.BlockSpec((bm, bn), lambda i, j: (i, j)),
        out_shape=jax.ShapeDtypeStruct((t, n), F32),
        compiler_params=_params("parallel", "parallel"),
        name="proj_gate",
    )(x_bf, w_g, b_gate)


def _swa_kernel(sink_ref, q_ref, kp_ref, kc_ref, vp_ref, vc_ref, o_ref):
    n = pl.program_id(1)
    grp = N_Q_HEADS // N_KV_HEADS
    q = q_ref[...]
    kk = jnp.concatenate([kp_ref[...], kc_ref[...]], axis=0)
    vv = jnp.concatenate([vp_ref[...], vc_ref[...]], axis=0)
    q_loc = lax.broadcasted_iota(jnp.int32, (WINDOW, 2 * WINDOW), 0)
    k_loc = lax.broadcasted_iota(jnp.int32, (WINDOW, 2 * WINDOW), 1)
    first_key = jnp.where(n > 0, 0, WINDOW)
    valid = (k_loc <= q_loc + WINDOW) & (k_loc > q_loc) & (k_loc >= first_key)
    scale = HEAD_DIM ** -0.5
    outs = []
    for g in range(N_KV_HEADS):
        kg = kk[:, g * HEAD_DIM:(g + 1) * HEAD_DIM]
        vg = vv[:, g * HEAD_DIM:(g + 1) * HEAD_DIM]
        for j in range(grp):
            h = g * grp + j
            sink = sink_ref[h]
            s = _dot_nt(q[:, h * HEAD_DIM:(h + 1) * HEAD_DIM], kg) * scale
            s = jnp.where(valid, s, -jnp.inf)
            m = jnp.maximum(jnp.max(s, axis=-1, keepdims=True), sink)
            p = jnp.exp(s - m)
            denom = jnp.sum(p, axis=-1, keepdims=True) + jnp.exp(sink - m)
            outs.append(_dot(p.astype(BF16), vg) / denom)
    o_ref[...] = jnp.concatenate(outs, axis=1).astype(BF16)


def _swa(q, k, v, sinks, bsz, seq):
    n_blk = seq // WINDOW
    cur = lambda b, n: (b * n_blk + n, 0)
    prev = lambda b, n: (b * n_blk + jnp.maximum(n - 1, 0), 0)
    return pl.pallas_call(
        _swa_kernel,
        grid=(bsz, n_blk),
        in_specs=[
            pl.BlockSpec(memory_space=pltpu.SMEM),
            pl.BlockSpec((WINDOW, ATTN_WIDTH), cur),
            pl.BlockSpec((WINDOW, KV_WIDTH), prev),
            pl.BlockSpec((WINDOW, KV_WIDTH), cur),
            pl.BlockSpec((WINDOW, KV_WIDTH), prev),
            pl.BlockSpec((WINDOW, KV_WIDTH), cur),
        ],
        out_specs=pl.BlockSpec((WINDOW, ATTN_WIDTH), cur),
        out_shape=jax.ShapeDtypeStruct((bsz * seq, ATTN_WIDTH), BF16),
        compiler_params=_params("parallel", "parallel"),
        name="swa",
    )(sinks, q, k, k, v, v)


def _mix_kernel(gu_ref, vn_ref, at_ref, ga_ref, gb_ref, x_ref, ws_ref, bs_ref, wa_ref, wb_ref, wo_ref,
                g_ref, b_ref, o_ref, obf_ref, gated_ref):
    bm = gu_ref.shape[0]
    t_idx = lax.broadcasted_iota(jnp.int32, (CHUNK, CHUNK), 0)
    s_idx = lax.broadcasted_iota(jnp.int32, (CHUNK, CHUNK), 1)
    causal = s_idx <= t_idx
    bs = bs_ref[...]
    for g in range(GMLP_GROUPS):
        w = jnp.where(causal, ws_ref[g], 0.0).astype(BF16)
        bias = bs[:, g:g + 1]
        cols = slice(g * LANES, (g + 1) * LANES)
        for c in range(bm // CHUNK):
            rows = slice(c * CHUNK, (c + 1) * CHUNK)
            mixed = _dot(w, vn_ref[rows, cols]) + bias
            gated_ref[rows, cols] = (gu_ref[rows, cols] * mixed).astype(BF16)
    ya = _dot(gated_ref[...], wa_ref[...])
    yb = _dot(at_ref[...], wb_ref[...])
    merged = (ga_ref[...] * ya + gb_ref[...] * yb).astype(BF16)
    z = ALPHA * x_ref[...] + _dot(merged, wo_ref[...])
    y = _layer_norm(z, g_ref[...], b_ref[...])
    o_ref[...] = y
    obf_ref[...] = y.astype(BF16)


def _mix(gu, vn, attn, gates, x, w_s, b_s_t, w_a, w_b, w_o, ln_g, ln_b, bm):
    t = x.shape[0]
    row = lambda i: (i, 0)
    const2 = lambda i: (0, 0)
    return pl.pallas_call(
        _mix_kernel,
        grid=(t // bm,),
        in_specs=[
            pl.BlockSpec((bm, GMLP_WIDTH), row),
            pl.BlockSpec((bm, GMLP_WIDTH), row),
            pl.BlockSpec((bm, ATTN_WIDTH), row),
            pl.BlockSpec((bm, D_MODEL), lambda i: (i, 0)),
            pl.BlockSpec((bm, D_MODEL), lambda i: (i, 1)),
            pl.BlockSpec((bm, D_MODEL), row),
            pl.BlockSpec((GMLP_GROUPS, CHUNK, CHUNK), lambda i: (0, 0, 0)),
            pl.BlockSpec((CHUNK, GMLP_GROUPS), const2),
            pl.BlockSpec((GMLP_WIDTH, D_MODEL), const2),
            pl.BlockSpec((ATTN_WIDTH, D_MODEL), const2),
            pl.BlockSpec((D_MODEL, D_MODEL), const2),
            pl.BlockSpec((1, D_MODEL), const2),
            pl.BlockSpec((1, D_MODEL), const2),
        ],
        out_specs=[pl.BlockSpec((bm, D_MODEL), row), pl.BlockSpec((bm, D_MODEL), row)],
        out_shape=[jax.ShapeDtypeStruct((t, D_MODEL), F32), jax.ShapeDtypeStruct((t, D_MODEL), BF16)],
        scratch_shapes=[pltpu.VMEM((bm, GMLP_WIDTH), BF16)],
        compiler_params=_params("parallel"),
        name="mix_out",
    )(gu, vn, attn, gates, gates, x, w_s, b_s_t, w_a, w_b, w_o, ln_g, ln_b)


def _mem_kv_kernel(m_ref, w_ref, o_ref):
    o_ref[...] = _dot(m_ref[...].astype(BF16), w_ref[...]).astype(BF16)


def _mem_kv(mem2d, w_xkv):
    rows = mem2d.shape[0]
    return pl.pallas_call(
        _mem_kv_kernel,
        grid=(2,),
        in_specs=[
            pl.BlockSpec((rows, D_MODEL), lambda j: (0, 0)),
            pl.BlockSpec((D_MODEL, X_WIDTH), lambda j: (0, j)),
        ],
        out_specs=pl.BlockSpec((rows, X_WIDTH), lambda j: (0, j)),
        out_shape=jax.ShapeDtypeStruct((rows, 2 * X_WIDTH), BF16),
        compiler_params=_params("parallel"),
        name="mem_kv",
    )(mem2d, w_xkv)


def _xattn_kernel(xbf_ref, x_ref, k_ref, v_ref, wq_ref, wo_ref, g_ref, b_ref, o_ref, obf_ref):
    q = _dot(xbf_ref[...], wq_ref[...]).astype(BF16)
    scale = X_HEAD_DIM ** -0.5
    outs = []
    for h in range(X_HEADS):
        cols = slice(h * X_HEAD_DIM, (h + 1) * X_HEAD_DIM)
        s = _dot_nt(q[:, cols], k_ref[:, cols]) * scale
        m = jnp.max(s, axis=-1, keepdims=True)
        p = jnp.exp(s - m)
        denom = jnp.sum(p, axis=-1, keepdims=True)
        outs.append(_dot(p.astype(BF16), v_ref[:, cols]) / denom)
    o = jnp.concatenate(outs, axis=1).astype(BF16)
    z = ALPHA * x_ref[...] + _dot(o, wo_ref[...])
    y = _layer_norm(z, g_ref[...], b_ref[...])
    o_ref[...] = y
    obf_ref[...] = y.astype(BF16)


def _xattn(x_bf, x, kv, w_xq, w_xo, ln_g, ln_b, bm, bsz, seq):
    t = x.shape[0]
    n_seq_blocks = seq // bm
    row = lambda b, i: (b * n_seq_blocks + i, 0)
    const2 = lambda b, i: (0, 0)
    return pl.pallas_call(
        _xattn_kernel,
        grid=(bsz, n_seq_blocks),
        in_specs=[
            pl.BlockSpec((bm, D_MODEL), row),
            pl.BlockSpec((bm, D_MODEL), row),
            pl.BlockSpec((MEM_LEN, X_WIDTH), lambda b, i: (b, 0)),
            pl.BlockSpec((MEM_LEN, X_WIDTH), lambda b, i: (b, 1)),
            pl.BlockSpec((D_MODEL, X_WIDTH), const2),
            pl.BlockSpec((X_WIDTH, D_MODEL), const2),
            pl.BlockSpec((1, D_MODEL), const2),
            pl.BlockSpec((1, D_MODEL), const2),
        ],
        out_specs=[pl.BlockSpec((bm, D_MODEL), row), pl.BlockSpec((bm, D_MODEL), row)],
        out_shape=[jax.ShapeDtypeStruct((t, D_MODEL), F32), jax.ShapeDtypeStruct((t, D_MODEL), BF16)],
        compiler_params=_params("parallel", "parallel"),
        name="xattn",
    )(x_bf, x, kv, kv, w_xq, w_xo, ln_g, ln_b)


def _mlp_kernel(xbf_ref, x_ref, wu_ref, wd_ref, g_ref, b_ref, o_ref, obf_ref, acc_ref):
    f = pl.program_id(1)
    h = jnp.maximum(_dot(xbf_ref[...], wu_ref[...]), 0.0)
    contrib = _dot((h * h).astype(BF16), wd_ref[...])

    @pl.when(f == 0)
    def _():
        acc_ref[...] = contrib

    @pl.when(f > 0)
    def _():
        acc_ref[...] += contrib

    @pl.when(f == pl.num_programs(1) - 1)
    def _():
        z = ALPHA * x_ref[...] + acc_ref[...]
        y = _layer_norm(z, g_ref[...], b_ref[...])
        o_ref[...] = y
        obf_ref[...] = y.astype(BF16)


def _mlp(x_bf, x, w_up, w_down, ln_g, ln_b, bm, bf):
    t = x.shape[0]
    row = lambda i, f: (i, 0)
    const2 = lambda i, f: (0, 0)
    return pl.pallas_call(
        _mlp_kernel,
        grid=(t // bm, D_FF // bf),
        in_specs=[
            pl.BlockSpec((bm, D_MODEL), row),
            pl.BlockSpec((bm, D_MODEL), row),
            pl.BlockSpec((D_MODEL, bf), lambda i, f: (0, f)),
            pl.BlockSpec((bf, D_MODEL), lambda i, f: (f, 0)),
            pl.BlockSpec((1, D_MODEL), const2),
            pl.BlockSpec((1, D_MODEL), const2),
        ],
        out_specs=[pl.BlockSpec((bm, D_MODEL), row), pl.BlockSpec((bm, D_MODEL), row)],
        out_shape=[jax.ShapeDtypeStruct((t, D_MODEL), F32), jax.ShapeDtypeStruct((t, D_MODEL), BF16)],
        scratch_shapes=[pltpu.VMEM((bm, D_MODEL), F32)],
        compiler_params=_params("parallel", "arbitrary"),
        name="mlp",
    )(x_bf, x, w_up, w_down, ln_g, ln_b)


def _rope_tables(seq):
    inv = 1.0 / (ROPE_THETA ** (jnp.arange(0, HEAD_DIM, 2, dtype=F32) / HEAD_DIM))
    ang = jnp.arange(seq, dtype=F32)[:, None] * inv[None, :]
    cos, sin = jnp.cos(ang), jnp.sin(ang)
    reps = LANES // HEAD_DIM
    cos_t = jnp.tile(jnp.concatenate([cos, cos], axis=1), (1, reps))
    sin_t = jnp.tile(jnp.concatenate([-sin, sin], axis=1), (1, reps))
    return cos_t, sin_t


def kernel(x, mem, w_in, b_gate, ln_v_g, ln_v_b, w_s, b_s, sinks, w_br_a, w_br_b, w_o, ln1_g, ln1_b,
           w_xq, w_xkv, w_xo, ln2_g, ln2_b, w_up, w_down, ln3_g, ln3_b):
    bsz, seq, d = x.shape
    t = bsz * seq
    cos_t, sin_t = _rope_tables(seq)
    xf = x.reshape(t, d)
    xb = xf.astype(BF16)
    mem2d = mem.reshape(bsz * MEM_LEN, d)
    row = lambda a: a.reshape(1, -1)
    for l in range(DEPTH):
        w_in_bf = w_in[l].astype(BF16)
        gu, vn = _proj_uv(xb, w_in_bf[:, :OFF_V], row(ln_v_g[l]), row(ln_v_b[l]), bm=512)
        q, k, v = _proj_qkv(xb, w_in_bf[:, OFF_V:OFF_VA], cos_t, sin_t, bm=512, seq=seq)
        gates = _proj_gate(xb, w_in_bf[:, OFF_VA:], row(b_gate[l]), bm=1024, bn=1024)
        attn = _swa(q, k, v, sinks[l], bsz, seq)
        xf, xb = _mix(gu, vn, attn, gates, xf, w_s[l], b_s[l].T, w_br_a[l].astype(BF16),
                      w_br_b[l].astype(BF16), w_o[l].astype(BF16), row(ln1_g[l]), row(ln1_b[l]), bm=256)
        kv = _mem_kv(mem2d, w_xkv[l].astype(BF16))
        xf, xb = _xattn(xb, xf, kv, w_xq[l].astype(BF16), w_xo[l].astype(BF16), row(ln2_g[l]),
                        row(ln2_b[l]), bm=512, bsz=bsz, seq=seq)
        xf, xb = _mlp(xb, xf, w_up[l].astype(BF16), w_down[l].astype(BF16), row(ln3_g[l]), row(ln3_b[l]),
                      bm=512, bf=512)
    return xf.reshape(bsz, seq, d)
```

```python
import jax
import jax.numpy as jnp
from jax import lax
from jax.experimental import pallas as pl
from jax.experimental.pallas import tpu as pltpu

D_MODEL = 2048
DEPTH = 2
MEM_LEN = 256
CHUNK = 128
GMLP_WIDTH = 1024
GMLP_GROUPS = 8
N_Q_HEADS = 16
N_KV_HEADS = 4
HEAD_DIM = 64
ATTN_WIDTH = N_Q_HEADS * HEAD_DIM
KV_WIDTH = N_KV_HEADS * HEAD_DIM
WINDOW = 128
ROPE_THETA = 10000.0
X_HEADS = 4
X_HEAD_DIM = 128
X_WIDTH = X_HEADS * X_HEAD_DIM
D_FF = 4 * D_MODEL
LN_EPS = 1e-5
ALPHA = (2 * DEPTH) ** 0.25

OFF_V = GMLP_WIDTH
OFF_Q = 2 * GMLP_WIDTH
OFF_K = OFF_Q + ATTN_WIDTH
OFF_GATE = OFF_K + 2 * KV_WIDTH
GATE_WIDTH = 2 * D_MODEL

LANES = 128
VMEM_LIMIT_BYTES = 56 * 1024 * 1024
SCORE_SCALE = HEAD_DIM ** -0.5

BF16 = jnp.bfloat16
F32 = jnp.float32


def _params(*semantics):
    return pltpu.CompilerParams(dimension_semantics=semantics, vmem_limit_bytes=VMEM_LIMIT_BYTES)


def _dot(a, b):
    return jnp.dot(a, b, preferred_element_type=F32)


def _dot_nt(a, b):
    return lax.dot_general(a, b, (((1,), (1,)), ((), ())), preferred_element_type=F32)


def _layer_norm(z, g, b):
    mu = jnp.mean(z, axis=-1, keepdims=True)
    zc = z - mu
    var = jnp.mean(zc * zc, axis=-1, keepdims=True)
    return zc * lax.rsqrt(var + LN_EPS) * g + b


def _gelu(x):
    return jax.nn.gelu(x, approximate=True)


def _cast_weight_once(w_ref, wbf_ref, step):
    @pl.when(step == 0)
    def _():
        wbf_ref[...] = w_ref[...].astype(BF16)


def _w_in_spec(layer, width, col_block, grid_rank):
    if grid_rank == 1:
        index_map = lambda i: (layer, 0, col_block)
    else:
        index_map = lambda j, i: (layer, 0, col_block + j)
    return pl.BlockSpec((None, D_MODEL, width), index_map)


def _proj_u_kernel(x_ref, w_ref, gu_ref, wbf_ref):
    _cast_weight_once(w_ref, wbf_ref, pl.program_id(0))
    gu_ref[...] = _gelu(_dot(x_ref[...], wbf_ref[...]))


def _proj_v_kernel(x_ref, w_ref, g_ref, b_ref, vn_ref, wbf_ref):
    _cast_weight_once(w_ref, wbf_ref, pl.program_id(0))
    v = _gelu(_dot(x_ref[...], wbf_ref[...]))
    vn_ref[...] = _layer_norm(v, g_ref[...], b_ref[...]).astype(BF16)


def _proj_u(x_bf, w_in, layer, bm):
    t = x_bf.shape[0]
    return pl.pallas_call(
        _proj_u_kernel,
        grid=(t // bm,),
        in_specs=[
            pl.BlockSpec((bm, D_MODEL), lambda i: (i, 0)),
            _w_in_spec(layer, GMLP_WIDTH, 0, 1),
        ],
        out_specs=pl.BlockSpec((bm, GMLP_WIDTH), lambda i: (i, 0)),
        out_shape=jax.ShapeDtypeStruct((t, GMLP_WIDTH), F32),
        scratch_shapes=[pltpu.VMEM((D_MODEL, GMLP_WIDTH), BF16)],
        compiler_params=_params("arbitrary"),
        name="proj_u",
    )(x_bf, w_in)


def _proj_v(x_bf, w_in, layer, ln_g, ln_b, bm):
    t = x_bf.shape[0]
    return pl.pallas_call(
        _proj_v_kernel,
        grid=(t // bm,),
        in_specs=[
            pl.BlockSpec((bm, D_MODEL), lambda i: (i, 0)),
            _w_in_spec(layer, GMLP_WIDTH, OFF_V // GMLP_WIDTH, 1),
            pl.BlockSpec((1, GMLP_WIDTH), lambda i: (0, 0)),
            pl.BlockSpec((1, GMLP_WIDTH), lambda i: (0, 0)),
        ],
        out_specs=pl.BlockSpec((bm, GMLP_WIDTH), lambda i: (i, 0)),
        out_shape=jax.ShapeDtypeStruct((t, GMLP_WIDTH), BF16),
        scratch_shapes=[pltpu.VMEM((D_MODEL, GMLP_WIDTH), BF16)],
        compiler_params=_params("arbitrary"),
        name="proj_v",
    )(x_bf, w_in, ln_g, ln_b)


def _rope(x, cos, sin_signed, first_half):
    outs = []
    for c in range(x.shape[1] // LANES):
        xs = x[:, c * LANES:(c + 1) * LANES]
        swapped = jnp.where(first_half, pltpu.roll(xs, LANES - HEAD_DIM // 2, 1),
                            pltpu.roll(xs, HEAD_DIM // 2, 1))
        outs.append(xs * cos + swapped * sin_signed)
    return outs


def _duplicate_heads(cols, low_half):
    outs = []
    for xs in cols:
        other = pltpu.roll(xs, HEAD_DIM, 1)
        outs.append(jnp.where(low_half, xs, other))
        outs.append(jnp.where(low_half, other, xs))
    return outs


def _head_masks(shape):
    lane = lax.broadcasted_iota(jnp.int32, shape, 1)
    return (lane % HEAD_DIM) < (HEAD_DIM // 2), (lane % LANES) < HEAD_DIM


def _proj_q_kernel(x_ref, w_ref, cos_ref, sin_ref, q_ref, wbf_ref):
    _cast_weight_once(w_ref, wbf_ref, pl.program_id(0))
    acc = _dot(x_ref[...], wbf_ref[...])
    first_half, low_half = _head_masks(cos_ref.shape)
    cols = _rope(acc, cos_ref[...] * SCORE_SCALE, sin_ref[...] * SCORE_SCALE, first_half)
    padded = []
    for c in cols:
        padded.append(jnp.where(low_half, c, 0.0))
        padded.append(jnp.where(low_half, 0.0, c))
    q_ref[...] = jnp.concatenate(padded, axis=1).astype(BF16)


def _proj_kv_kernel(x_ref, w_ref, cos_ref, sin_ref, k_ref, v_ref, wbf_ref):
    _cast_weight_once(w_ref, wbf_ref, pl.program_id(0))
    acc = _dot(x_ref[...], wbf_ref[...])
    first_half, low_half = _head_masks(cos_ref.shape)
    k_cols = _rope(acc[:, :KV_WIDTH], cos_ref[...], sin_ref[...], first_half)
    v_cols = [acc[:, KV_WIDTH + c * LANES:KV_WIDTH + (c + 1) * LANES] for c in range(KV_WIDTH // LANES)]
    k_ref[...] = jnp.concatenate(_duplicate_heads(k_cols, low_half), axis=1).astype(BF16)
    v_ref[...] = jnp.concatenate(_duplicate_heads(v_cols, low_half), axis=1).astype(BF16)


def _proj_q(x_bf, w_in, layer, cos_t, sin_t, bm, seq):
    t = x_bf.shape[0]
    tiles_per_seq = seq // bm
    return pl.pallas_call(
        _proj_q_kernel,
        grid=(t // bm,),
        in_specs=[
            pl.BlockSpec((bm, D_MODEL), lambda i: (i, 0)),
            _w_in_spec(layer, ATTN_WIDTH, OFF_Q // ATTN_WIDTH, 1),
            pl.BlockSpec((bm, LANES), lambda i: (i % tiles_per_seq, 0)),
            pl.BlockSpec((bm, LANES), lambda i: (i % tiles_per_seq, 0)),
        ],
        out_specs=pl.BlockSpec((bm, 2 * ATTN_WIDTH), lambda i: (i, 0)),
        out_shape=jax.ShapeDtypeStruct((t, 2 * ATTN_WIDTH), BF16),
        scratch_shapes=[pltpu.VMEM((D_MODEL, ATTN_WIDTH), BF16)],
        compiler_params=_params("arbitrary"),
        name="proj_q",
    )(x_bf, w_in, cos_t, sin_t)


def _proj_kv(x_bf, w_in, layer, cos_t, sin_t, bm, seq):
    t = x_bf.shape[0]
    tiles_per_seq = seq // bm
    return pl.pallas_call(
        _proj_kv_kernel,
        grid=(t // bm,),
        in_specs=[
            pl.BlockSpec((bm, D_MODEL), lambda i: (i, 0)),
            _w_in_spec(layer, 2 * KV_WIDTH, OFF_K // (2 * KV_WIDTH), 1),
            pl.BlockSpec((bm, LANES), lambda i: (i % tiles_per_seq, 0)),
            pl.BlockSpec((bm, LANES), lambda i: (i % tiles_per_seq, 0)),
        ],
        out_specs=[
            pl.BlockSpec((bm, 2 * KV_WIDTH), lambda i: (i, 0)),
            pl.BlockSpec((bm, 2 * KV_WIDTH), lambda i: (i, 0)),
        ],
        out_shape=[
            jax.ShapeDtypeStruct((t, 2 * KV_WIDTH), BF16),
            jax.ShapeDtypeStruct((t, 2 * KV_WIDTH), BF16),
        ],
        scratch_shapes=[pltpu.VMEM((D_MODEL, 2 * KV_WIDTH), BF16)],
        compiler_params=_params("arbitrary"),
        name="proj_kv",
    )(x_bf, w_in, cos_t, sin_t)


def _proj_gate_kernel(x_ref, w_ref, b_ref, o_ref, wbf_ref):
    _cast_weight_once(w_ref, wbf_ref, pl.program_id(1))
    o_ref[...] = jax.nn.sigmoid(_dot(x_ref[...], wbf_ref[...]) + b_ref[...])


def _proj_gate(x_bf, w_in, layer, b_gate, bm, bn):
    t = x_bf.shape[0]
    return pl.pallas_call(
        _proj_gate_kernel,
        grid=(GATE_WIDTH // bn, t // bm),
        in_specs=[
            pl.BlockSpec((bm, D_MODEL), lambda j, i: (i, 0)),
            _w_in_spec(layer, bn, OFF_GATE // bn, 2),
            pl.BlockSpec((1, bn), lambda j, i: (0, j)),
        ],
        out_specs=pl.BlockSpec((bm, bn), lambda j, i: (i, j)),
        out_shape=jax.ShapeDtypeStruct((t, GATE_WIDTH), F32),
        scratch_shapes=[pltpu.VMEM((D_MODEL, bn), BF16)],
        compiler_params=_params("arbitrary", "arbitrary"),
        name="proj_gate",
    )(x_bf, w_in, b_gate)


def _swa_kernel(sink_ref, q_ref, kp_ref, kc_ref, vp_ref, vc_ref, o_ref):
    n = pl.program_id(1)
    pair_rows = 2 * WINDOW
    lane = lax.broadcasted_iota(jnp.int32, (WINDOW, LANES), 1)
    low_half = lane < HEAD_DIM
    row = lax.broadcasted_iota(jnp.int32, (pair_rows, WINDOW), 0)
    key = lax.broadcasted_iota(jnp.int32, (pair_rows, WINDOW), 1)
    in_own_block = key <= (row % WINDOW)
    top_rows = lax.broadcasted_iota(jnp.int32, (pair_rows, 1), 0) < WINDOW
    first_key = jnp.where(n > 0, 0, WINDOW)
    prev_bias = jnp.where(lax.broadcasted_iota(jnp.int32, (1, WINDOW), 1) >= first_key, 0.0, -jnp.inf)
    outs = []
    for p in range(N_Q_HEADS // 2):
        g = (2 * p) // (N_Q_HEADS // N_KV_HEADS)
        kv_cols = slice(g * LANES, (g + 1) * LANES)
        qs = jnp.concatenate([q_ref[:, 2 * p * LANES:(2 * p + 1) * LANES],
                              q_ref[:, (2 * p + 1) * LANES:(2 * p + 2) * LANES]], axis=0)
        s_prev = _dot_nt(qs, kp_ref[:, kv_cols]) + prev_bias
        s = jnp.where(in_own_block, _dot_nt(qs, kc_ref[:, kv_cols]), s_prev)
        sink = jnp.where(top_rows, sink_ref[2 * p], sink_ref[2 * p + 1])
        m = jnp.maximum(jnp.max(s, axis=-1, keepdims=True), sink)
        e = jnp.exp(s - m)
        denom = jnp.sum(e, axis=-1, keepdims=True) + jnp.exp(sink - m)
        e_own = jnp.where(in_own_block, e, 0.0).astype(BF16)
        e_prev = jnp.where(in_own_block, 0.0, e).astype(BF16)
        o2 = (_dot(e_own, vc_ref[:, kv_cols]) + _dot(e_prev, vp_ref[:, kv_cols])) / denom
        outs.append(jnp.where(low_half, o2[:WINDOW], o2[WINDOW:]))
    o_ref[...] = jnp.concatenate(outs, axis=1).astype(BF16)


def _swa(q, k2, v2, sinks, bsz, seq):
    n_blk = seq // WINDOW
    cur = lambda b, n: (b * n_blk + n, 0)
    prev = lambda b, n: (b * n_blk + jnp.maximum(n - 1, 0), 0)
    return pl.pallas_call(
        _swa_kernel,
        grid=(bsz, n_blk),
        in_specs=[
            pl.BlockSpec(memory_space=pltpu.SMEM),
            pl.BlockSpec((WINDOW, 2 * ATTN_WIDTH), cur),
            pl.BlockSpec((WINDOW, 2 * KV_WIDTH), prev),
            pl.BlockSpec((WINDOW, 2 * KV_WIDTH), cur),
            pl.BlockSpec((WINDOW, 2 * KV_WIDTH), prev),
            pl.BlockSpec((WINDOW, 2 * KV_WIDTH), cur),
        ],
        out_specs=pl.BlockSpec((WINDOW, ATTN_WIDTH), cur),
        out_shape=jax.ShapeDtypeStruct((bsz * seq, ATTN_WIDTH), BF16),
        compiler_params=_params("parallel", "parallel"),
        name="swa",
    )(sinks, q, k2, k2, v2, v2)


def _mix_kernel(gu_ref, vn_ref, at_ref, ga_ref, gb_ref, x_ref, ws_ref, bs_ref, wa_ref, wb_ref, wo_ref,
                g_ref, b_ref, o_ref, obf_ref, gated_ref):
    bm = gu_ref.shape[0]
    t_idx = lax.broadcasted_iota(jnp.int32, (CHUNK, CHUNK), 0)
    s_idx = lax.broadcasted_iota(jnp.int32, (CHUNK, CHUNK), 1)
    causal = s_idx <= t_idx
    bs = bs_ref[...]
    for g in range(GMLP_GROUPS):
        w = jnp.where(causal, ws_ref[g], 0.0).astype(BF16)
        bias = bs[:, g:g + 1]
        cols = slice(g * LANES, (g + 1) * LANES)
        for c in range(bm // CHUNK):
            rows = slice(c * CHUNK, (c + 1) * CHUNK)
            mixed = _dot(w, vn_ref[rows, cols]) + bias
            gated_ref[rows, cols] = (gu_ref[rows, cols] * mixed).astype(BF16)
    ya = _dot(gated_ref[...], wa_ref[...])
    yb = _dot(at_ref[...], wb_ref[...])
    merged = (ga_ref[...] * ya + gb_ref[...] * yb).astype(BF16)
    z = ALPHA * x_ref[...] + _dot(merged, wo_ref[...])
    y = _layer_norm(z, g_ref[...], b_ref[...])
    o_ref[...] = y
    obf_ref[...] = y.astype(BF16)


def _mix(gu, vn, attn, gates, x, w_s, b_s_t, w_a, w_b, w_o, ln_g, ln_b, bm):
    t = x.shape[0]
    row = lambda i: (i, 0)
    const2 = lambda i: (0, 0)
    return pl.pallas_call(
        _mix_kernel,
        grid=(t // bm,),
        in_specs=[
            pl.BlockSpec((bm, GMLP_WIDTH), row),
            pl.BlockSpec((bm, GMLP_WIDTH), row),
            pl.BlockSpec((bm, ATTN_WIDTH), row),
            pl.BlockSpec((bm, D_MODEL), lambda i: (i, 0)),
            pl.BlockSpec((bm, D_MODEL), lambda i: (i, 1)),
            pl.BlockSpec((bm, D_MODEL), row),
            pl.BlockSpec((GMLP_GROUPS, CHUNK, CHUNK), lambda i: (0, 0, 0)),
            pl.BlockSpec((CHUNK, GMLP_GROUPS), const2),
            pl.BlockSpec((GMLP_WIDTH, D_MODEL), const2),
            pl.BlockSpec((ATTN_WIDTH, D_MODEL), const2),
            pl.BlockSpec((D_MODEL, D_MODEL), const2),
            pl.BlockSpec((1, D_MODEL), const2),
            pl.BlockSpec((1, D_MODEL), const2),
        ],
        out_specs=[pl.BlockSpec((bm, D_MODEL), row), pl.BlockSpec((bm, D_MODEL), row)],
        out_shape=[jax.ShapeDtypeStruct((t, D_MODEL), F32), jax.ShapeDtypeStruct((t, D_MODEL), BF16)],
        scratch_shapes=[pltpu.VMEM((bm, GMLP_WIDTH), BF16)],
        compiler_params=_params("parallel"),
        name="mix_out",
    )(gu, vn, attn, gates, gates, x, w_s, b_s_t, w_a, w_b, w_o, ln_g, ln_b)


def _mem_kv_kernel(m_ref, w_ref, o_ref):
    o_ref[...] = _dot(m_ref[...].astype(BF16), w_ref[...].astype(BF16)).astype(BF16)


def _mem_kv(mem2d, w_xkv, layer):
    rows = mem2d.shape[0]
    return pl.pallas_call(
        _mem_kv_kernel,
        grid=(2,),
        in_specs=[
            pl.BlockSpec((rows, D_MODEL), lambda j: (0, 0)),
            pl.BlockSpec((None, D_MODEL, X_WIDTH), lambda j: (layer, 0, j)),
        ],
        out_specs=pl.BlockSpec((rows, X_WIDTH), lambda j: (0, j)),
        out_shape=jax.ShapeDtypeStruct((rows, 2 * X_WIDTH), BF16),
        compiler_params=_params("parallel"),
        name="mem_kv",
    )(mem2d, w_xkv)


def _xattn_kernel(xbf_ref, x_ref, k_ref, v_ref, wq_ref, wo_ref, g_ref, b_ref, o_ref, obf_ref):
    q = _dot(xbf_ref[...], wq_ref[...]).astype(BF16)
    scale = X_HEAD_DIM ** -0.5
    outs = []
    for h in range(X_HEADS):
        cols = slice(h * X_HEAD_DIM, (h + 1) * X_HEAD_DIM)
        s = _dot_nt(q[:, cols], k_ref[:, cols]) * scale
        m = jnp.max(s, axis=-1, keepdims=True)
        p = jnp.exp(s - m)
        denom = jnp.sum(p, axis=-1, keepdims=True)
        outs.append(_dot(p.astype(BF16), v_ref[:, cols]) / denom)
    o = jnp.concatenate(outs, axis=1).astype(BF16)
    z = ALPHA * x_ref[...] + _dot(o, wo_ref[...])
    y = _layer_norm(z, g_ref[...], b_ref[...])
    o_ref[...] = y
    obf_ref[...] = y.astype(BF16)


def _xattn(x_bf, x, kv, w_xq, w_xo, ln_g, ln_b, bm, bsz, seq):
    t = x.shape[0]
    n_seq_blocks = seq // bm
    row = lambda b, i: (b * n_seq_blocks + i, 0)
    const2 = lambda b, i: (0, 0)
    return pl.pallas_call(
        _xattn_kernel,
        grid=(bsz, n_seq_blocks),
        in_specs=[
            pl.BlockSpec((bm, D_MODEL), row),
            pl.BlockSpec((bm, D_MODEL), row),
            pl.BlockSpec((MEM_LEN, X_WIDTH), lambda b, i: (b, 0)),
            pl.BlockSpec((MEM_LEN, X_WIDTH), lambda b, i: (b, 1)),
            pl.BlockSpec((D_MODEL, X_WIDTH), const2),
            pl.BlockSpec((X_WIDTH, D_MODEL), const2),
            pl.BlockSpec((1, D_MODEL), const2),
            pl.BlockSpec((1, D_MODEL), const2),
        ],
        out_specs=[pl.BlockSpec((bm, D_MODEL), row), pl.BlockSpec((bm, D_MODEL), row)],
        out_shape=[jax.ShapeDtypeStruct((t, D_MODEL), F32), jax.ShapeDtypeStruct((t, D_MODEL), BF16)],
        compiler_params=_params("parallel", "parallel"),
        name="xattn",
    )(x_bf, x, kv, kv, w_xq, w_xo, ln_g, ln_b)


def _mlp_kernel(xbf_ref, x_ref, wu_ref, wd_ref, g_ref, b_ref, o_ref, obf_ref, acc_ref):
    f = pl.program_id(1)

    @pl.when(f == 0)
    def _():
        acc_ref[...] = jnp.zeros_like(acc_ref)

    h = jnp.maximum(_dot(xbf_ref[...], wu_ref[...]), 0.0)
    acc_ref[...] += _dot((h * h).astype(BF16), wd_ref[...])

    @pl.when(f == pl.num_programs(1) - 1)
    def _():
        z = ALPHA * x_ref[...] + acc_ref[...]
        y = _layer_norm(z, g_ref[...], b_ref[...])
        o_ref[...] = y
        obf_ref[...] = y.astype(BF16)


def _mlp(x_bf, x, w_up, w_down, ln_g, ln_b, bm, bf):
    t = x.shape[0]
    row = lambda i, f: (i, 0)
    const2 = lambda i, f: (0, 0)
    return pl.pallas_call(
        _mlp_kernel,
        grid=(t // bm, D_FF // bf),
        in_specs=[
            pl.BlockSpec((bm, D_MODEL), row),
            pl.BlockSpec((bm, D_MODEL), row),
            pl.BlockSpec((D_MODEL, bf), lambda i, f: (0, f)),
            pl.BlockSpec((bf, D_MODEL), lambda i, f: (f, 0)),
            pl.BlockSpec((1, D_MODEL), const2),
            pl.BlockSpec((1, D_MODEL), const2),
        ],
        out_specs=[pl.BlockSpec((bm, D_MODEL), row), pl.BlockSpec((bm, D_MODEL), row)],
        out_shape=[jax.ShapeDtypeStruct((t, D_MODEL), F32), jax.ShapeDtypeStruct((t, D_MODEL), BF16)],
        scratch_shapes=[pltpu.VMEM((bm, D_MODEL), F32)],
        compiler_params=_params("parallel", "arbitrary"),
        name="mlp",
    )(x_bf, x, w_up, w_down, ln_g, ln_b)


def _rope_tables(seq):
    inv = 1.0 / (ROPE_THETA ** (jnp.arange(0, HEAD_DIM, 2, dtype=F32) / HEAD_DIM))
    ang = jnp.arange(seq, dtype=F32)[:, None] * inv[None, :]
    cos, sin = jnp.cos(ang), jnp.sin(ang)
    reps = LANES // HEAD_DIM
    cos_t = jnp.tile(jnp.concatenate([cos, cos], axis=1), (1, reps))
    sin_t = jnp.tile(jnp.concatenate([-sin, sin], axis=1), (1, reps))
    return cos_t, sin_t


def kernel(x, mem, w_in, b_gate, ln_v_g, ln_v_b, w_s, b_s, sinks, w_br_a, w_br_b, w_o, ln1_g, ln1_b,
           w_xq, w_xkv, w_xo, ln2_g, ln2_b, w_up, w_down, ln3_g, ln3_b):
    bsz, seq, d = x.shape
    t = bsz * seq
    cos_t, sin_t = _rope_tables(seq)
    xf = x.reshape(t, d)
    xb = xf.astype(BF16)
    mem2d = mem.reshape(bsz * MEM_LEN, d)
    row = lambda a: a.reshape(1, -1)
    for l in range(DEPTH):
        gu = _proj_u(xb, w_in, l, bm=1024)
        vn = _proj_v(xb, w_in, l, row(ln_v_g[l]), row(ln_v_b[l]), bm=1024)
        q = _proj_q(xb, w_in, l, cos_t, sin_t, bm=1024, seq=seq)
        k2, v2 = _proj_kv(xb, w_in, l, cos_t, sin_t, bm=1024, seq=seq)
        gates = _proj_gate(xb, w_in, l, row(b_gate[l]), bm=1024, bn=512)
        attn = _swa(q, k2, v2, sinks[l], bsz, seq)
        xf, xb = _mix(gu, vn, attn, gates, xf, w_s[l], b_s[l].T, w_br_a[l].astype(BF16),
                      w_br_b[l].astype(BF16), w_o[l].astype(BF16), row(ln1_g[l]), row(ln1_b[l]), bm=256)
        kv = _mem_kv(mem2d, w_xkv, l)
        xf, xb = _xattn(xb, xf, kv, w_xq[l].astype(BF16), w_xo[l].astype(BF16), row(ln2_g[l]),
                        row(ln2_b[l]), bm=512, bsz=bsz, seq=seq)
        xf, xb = _mlp(xb, xf, w_up[l].astype(BF16), w_down[l].astype(BF16), row(ln3_g[l]), row(ln3_b[l]),
                      bm=512, bf=1024)
    return xf.reshape(bsz, seq, d)
```

```python
import functools

import jax
import jax.numpy as jnp
from jax import lax
from jax.experimental import pallas as pl
from jax.experimental.pallas import tpu as pltpu

D_MODEL = 2048
DEPTH = 2
MEM_LEN = 256
CHUNK = 128
GMLP_WIDTH = 1024
GMLP_GROUPS = 8
N_Q_HEADS = 16
N_KV_HEADS = 4
HEAD_DIM = 64
ATTN_WIDTH = N_Q_HEADS * HEAD_DIM
KV_WIDTH = N_KV_HEADS * HEAD_DIM
WINDOW = 128
ROPE_THETA = 10000.0
X_HEADS = 4
X_HEAD_DIM = 128
X_WIDTH = X_HEADS * X_HEAD_DIM
D_FF = 4 * D_MODEL
LN_EPS = 1e-5
ALPHA = (2 * DEPTH) ** 0.25

OFF_V = GMLP_WIDTH
OFF_Q = 2 * GMLP_WIDTH
OFF_K = OFF_Q + ATTN_WIDTH
OFF_GATE = OFF_K + 2 * KV_WIDTH
GATE_WIDTH = 2 * D_MODEL

LANES = 128
VMEM_LIMIT_BYTES = 56 * 1024 * 1024
SWA_BLOCKS = 2
EPILOGUE_ROWS = 32
SCORE_SCALE = HEAD_DIM ** -0.5

BF16 = jnp.bfloat16
F32 = jnp.float32


def _params(*semantics):
    return pltpu.CompilerParams(dimension_semantics=semantics, vmem_limit_bytes=VMEM_LIMIT_BYTES)


def _dot(a, b):
    return jnp.dot(a, b, preferred_element_type=F32)


def _dot_nt(a, b):
    return lax.dot_general(a, b, (((1,), (1,)), ((), ())), preferred_element_type=F32)


def _layer_norm(z, g, b):
    mu = jnp.mean(z, axis=-1, keepdims=True)
    zc = z - mu
    var = jnp.mean(zc * zc, axis=-1, keepdims=True)
    return zc * lax.rsqrt(var + LN_EPS) * g + b


def _gelu(x):
    return jax.nn.gelu(x, approximate=True)


def _cast_weight_once(w_ref, wbf_ref, step):
    @pl.when(step == 0)
    def _():
        wbf_ref[...] = w_ref[...].astype(BF16)


def _deferred_epilogue(step, n_tiles, acc_refs, compute, epilogue):
    def run_epilogue(acc_ref):
        for r in range(0, acc_ref.shape[0], EPILOGUE_ROWS):
            epilogue(acc_ref, slice(r, r + EPILOGUE_ROWS))

    @pl.when(step == 0)
    def _():
        acc_refs[0][...] = compute()

    for parity in range(2):
        @pl.when((step > 0) & (step < n_tiles) & (step % 2 == parity))
        def _():
            run_epilogue(acc_refs[1 - parity])
            acc_refs[parity][...] = compute()

    @pl.when(step == n_tiles)
    def _():
        run_epilogue(acc_refs[1 - n_tiles % 2])


def _cur_tile(n_tiles):
    return lambda s: (jnp.minimum(s, n_tiles - 1), 0)


def _done_tile(s):
    return (jnp.maximum(s - 1, 0), 0)


def _acc_scratch(bm, width):
    return [pltpu.VMEM((bm, width), F32), pltpu.VMEM((bm, width), F32)]


def _w_in_spec(layer, width, col_block):
    return pl.BlockSpec((None, D_MODEL, width), lambda s: (layer, 0, col_block))


def _proj_u_kernel(x_ref, w_ref, gu_ref, wbf_ref):
    _cast_weight_once(w_ref, wbf_ref, pl.program_id(0))
    gu_ref[...] = _gelu(_dot(x_ref[...], wbf_ref[...]))


def _proj_v_kernel(x_ref, w_ref, g_ref, b_ref, vn_ref, wbf_ref):
    _cast_weight_once(w_ref, wbf_ref, pl.program_id(0))
    v = _gelu(_dot(x_ref[...], wbf_ref[...]))
    vn_ref[...] = _layer_norm(v, g_ref[...], b_ref[...]).astype(BF16)


def _proj_u(x_bf, w_in, layer, bm):
    t = x_bf.shape[0]
    return pl.pallas_call(
        _proj_u_kernel,
        grid=(t // bm,),
        in_specs=[
            pl.BlockSpec((bm, D_MODEL), lambda i: (i, 0)),
            _w_in_spec(layer, GMLP_WIDTH, 0),
        ],
        out_specs=pl.BlockSpec((bm, GMLP_WIDTH), lambda i: (i, 0)),
        out_shape=jax.ShapeDtypeStruct((t, GMLP_WIDTH), F32),
        scratch_shapes=[pltpu.VMEM((D_MODEL, GMLP_WIDTH), BF16)],
        compiler_params=_params("arbitrary"),
        name="proj_u",
    )(x_bf, w_in)


def _proj_v(x_bf, w_in, layer, ln_g, ln_b, bm):
    t = x_bf.shape[0]
    return pl.pallas_call(
        _proj_v_kernel,
        grid=(t // bm,),
        in_specs=[
            pl.BlockSpec((bm, D_MODEL), lambda i: (i, 0)),
            _w_in_spec(layer, GMLP_WIDTH, OFF_V // GMLP_WIDTH),
            pl.BlockSpec((1, GMLP_WIDTH), lambda i: (0, 0)),
            pl.BlockSpec((1, GMLP_WIDTH), lambda i: (0, 0)),
        ],
        out_specs=pl.BlockSpec((bm, GMLP_WIDTH), lambda i: (i, 0)),
        out_shape=jax.ShapeDtypeStruct((t, GMLP_WIDTH), BF16),
        scratch_shapes=[pltpu.VMEM((D_MODEL, GMLP_WIDTH), BF16)],
        compiler_params=_params("arbitrary"),
        name="proj_v",
    )(x_bf, w_in, ln_g, ln_b)


def _rope(x, cos, sin_signed, first_half):
    outs = []
    for c in range(x.shape[1] // LANES):
        xs = x[:, c * LANES:(c + 1) * LANES]
        swapped = jnp.where(first_half, pltpu.roll(xs, LANES - HEAD_DIM // 2, 1),
                            pltpu.roll(xs, HEAD_DIM // 2, 1))
        outs.append(xs * cos + swapped * sin_signed)
    return outs


def _duplicate_heads(cols, low_half):
    outs = []
    for xs in cols:
        other = pltpu.roll(xs, HEAD_DIM, 1)
        outs.append(jnp.where(low_half, xs, other))
        outs.append(jnp.where(low_half, other, xs))
    return outs


def _head_masks(shape):
    lane = lax.broadcasted_iota(jnp.int32, shape, 1)
    return (lane % HEAD_DIM) < (HEAD_DIM // 2), (lane % LANES) < HEAD_DIM


def _proj_q_kernel(n_tiles, x_ref, w_ref, cos_ref, sin_ref, q_ref, wbf_ref, acc0_ref, acc1_ref):
    step = pl.program_id(0)
    _cast_weight_once(w_ref, wbf_ref, step)

    def epilogue(acc_ref, rows):
        first_half, low_half = _head_masks((EPILOGUE_ROWS, LANES))
        cols = _rope(acc_ref[rows, :], cos_ref[rows, :] * SCORE_SCALE, sin_ref[rows, :] * SCORE_SCALE, first_half)
        padded = []
        for c in cols:
            padded.append(jnp.where(low_half, c, 0.0))
            padded.append(jnp.where(low_half, 0.0, c))
        q_ref[rows, :] = jnp.concatenate(padded, axis=1).astype(BF16)

    _deferred_epilogue(step, n_tiles, (acc0_ref, acc1_ref), lambda: _dot(x_ref[...], wbf_ref[...]), epilogue)


def _proj_kv_kernel(n_tiles, x_ref, w_ref, cos_ref, sin_ref, k_ref, v_ref, wbf_ref, acc0_ref, acc1_ref):
    step = pl.program_id(0)
    _cast_weight_once(w_ref, wbf_ref, step)

    def epilogue(acc_ref, rows):
        first_half, low_half = _head_masks((EPILOGUE_ROWS, LANES))
        acc = acc_ref[rows, :]
        k_cols = _rope(acc[:, :KV_WIDTH], cos_ref[rows, :], sin_ref[rows, :], first_half)
        v_cols = [acc[:, KV_WIDTH + c * LANES:KV_WIDTH + (c + 1) * LANES] for c in range(KV_WIDTH // LANES)]
        k_ref[rows, :] = jnp.concatenate(_duplicate_heads(k_cols, low_half), axis=1).astype(BF16)
        v_ref[rows, :] = jnp.concatenate(_duplicate_heads(v_cols, low_half), axis=1).astype(BF16)

    _deferred_epilogue(step, n_tiles, (acc0_ref, acc1_ref), lambda: _dot(x_ref[...], wbf_ref[...]), epilogue)


def _rope_table_spec(bm, tiles_per_seq):
    return pl.BlockSpec((bm, LANES), lambda s: (jnp.maximum(s - 1, 0) % tiles_per_seq, 0))


def _proj_q(x_bf, w_in, layer, cos_t, sin_t, bm, seq):
    t = x_bf.shape[0]
    n_tiles = t // bm
    return pl.pallas_call(
        functools.partial(_proj_q_kernel, n_tiles),
        grid=(n_tiles + 1,),
        in_specs=[
            pl.BlockSpec((bm, D_MODEL), _cur_tile(n_tiles)),
            _w_in_spec(layer, ATTN_WIDTH, OFF_Q // ATTN_WIDTH),
            _rope_table_spec(bm, seq // bm),
            _rope_table_spec(bm, seq // bm),
        ],
        out_specs=pl.BlockSpec((bm, 2 * ATTN_WIDTH), _done_tile),
        out_shape=jax.ShapeDtypeStruct((t, 2 * ATTN_WIDTH), BF16),
        scratch_shapes=[pltpu.VMEM((D_MODEL, ATTN_WIDTH), BF16)] + _acc_scratch(bm, ATTN_WIDTH),
        compiler_params=_params("arbitrary"),
        name="proj_q",
    )(x_bf, w_in, cos_t, sin_t)


def _proj_kv(x_bf, w_in, layer, cos_t, sin_t, bm, seq):
    t = x_bf.shape[0]
    n_tiles = t // bm
    return pl.pallas_call(
        functools.partial(_proj_kv_kernel, n_tiles),
        grid=(n_tiles + 1,),
        in_specs=[
            pl.BlockSpec((bm, D_MODEL), _cur_tile(n_tiles)),
            _w_in_spec(layer, 2 * KV_WIDTH, OFF_K // (2 * KV_WIDTH)),
            _rope_table_spec(bm, seq // bm),
            _rope_table_spec(bm, seq // bm),
        ],
        out_specs=[
            pl.BlockSpec((bm, 2 * KV_WIDTH), _done_tile),
            pl.BlockSpec((bm, 2 * KV_WIDTH), _done_tile),
        ],
        out_shape=[
            jax.ShapeDtypeStruct((t, 2 * KV_WIDTH), BF16),
            jax.ShapeDtypeStruct((t, 2 * KV_WIDTH), BF16),
        ],
        scratch_shapes=[pltpu.VMEM((D_MODEL, 2 * KV_WIDTH), BF16)] + _acc_scratch(bm, 2 * KV_WIDTH),
        compiler_params=_params("arbitrary"),
        name="proj_kv",
    )(x_bf, w_in, cos_t, sin_t)


def _proj_gate_kernel(n_tiles, m_tiles, x_ref, w0_ref, w1_ref, b_ref, o_ref, wbf_ref, acc0_ref, acc1_ref):
    step = pl.program_id(0)
    half = w0_ref.shape[1]

    @pl.when((step % m_tiles == 0) & (step < n_tiles))
    def _():
        wbf_ref[:, :half] = w0_ref[...].astype(BF16)
        wbf_ref[:, half:] = w1_ref[...].astype(BF16)

    def epilogue(acc_ref, rows):
        o_ref[rows, :] = jax.nn.sigmoid(acc_ref[rows, :] + b_ref[...])

    _deferred_epilogue(step, n_tiles, (acc0_ref, acc1_ref), lambda: _dot(x_ref[...], wbf_ref[...]), epilogue)


def _proj_gate(x_bf, w_in, layer, b_gate, bm, bn):
    t = x_bf.shape[0]
    m_tiles = t // bm
    n_tiles = m_tiles * (GATE_WIDTH // bn)
    half = bn // 2
    first = OFF_GATE // half
    cur = lambda s: jnp.minimum(s, n_tiles - 1)
    done = lambda s: jnp.maximum(s - 1, 0)
    return pl.pallas_call(
        functools.partial(_proj_gate_kernel, n_tiles, m_tiles),
        grid=(n_tiles + 1,),
        in_specs=[
            pl.BlockSpec((bm, D_MODEL), lambda s: (cur(s) % m_tiles, 0)),
            pl.BlockSpec((None, D_MODEL, half), lambda s: (layer, 0, first + 2 * (cur(s) // m_tiles))),
            pl.BlockSpec((None, D_MODEL, half), lambda s: (layer, 0, first + 2 * (cur(s) // m_tiles) + 1)),
            pl.BlockSpec((1, bn), lambda s: (0, done(s) // m_tiles)),
        ],
        out_specs=pl.BlockSpec((bm, bn), lambda s: (done(s) % m_tiles, done(s) // m_tiles)),
        out_shape=jax.ShapeDtypeStruct((t, GATE_WIDTH), F32),
        scratch_shapes=[pltpu.VMEM((D_MODEL, bn), BF16)] + _acc_scratch(bm, bn),
        compiler_params=_params("arbitrary"),
        name="proj_gate",
    )(x_bf, w_in, w_in, b_gate)


def _swa_kernel(sink_ref, q_ref, kp_ref, kc_ref, vp_ref, vc_ref, o_ref):
    n = pl.program_id(1)
    pair_rows = 2 * WINDOW
    lane = lax.broadcasted_iota(jnp.int32, (WINDOW, LANES), 1)
    low_half = lane < HEAD_DIM
    row = lax.broadcasted_iota(jnp.int32, (pair_rows, WINDOW), 0)
    key = lax.broadcasted_iota(jnp.int32, (pair_rows, WINDOW), 1)
    in_own_block = key <= (row % WINDOW)
    top_rows = lax.broadcasted_iota(jnp.int32, (pair_rows, 1), 0) < WINDOW
    first_key = jnp.where(n > 0, 0, WINDOW)
    first_bias = jnp.where(lax.broadcasted_iota(jnp.int32, (1, WINDOW), 1) >= first_key, 0.0, -jnp.inf)
    for blk in range(SWA_BLOCKS):
        own = slice(blk * WINDOW, (blk + 1) * WINDOW)
        outs = []
        for p in range(N_Q_HEADS // 2):
            g = (2 * p) // (N_Q_HEADS // N_KV_HEADS)
            kv_cols = slice(g * LANES, (g + 1) * LANES)
            if blk == 0:
                k_prev, v_prev = kp_ref[:, kv_cols], vp_ref[:, kv_cols]
            else:
                before = slice((blk - 1) * WINDOW, blk * WINDOW)
                k_prev, v_prev = kc_ref[before, kv_cols], vc_ref[before, kv_cols]
            qs = jnp.concatenate([q_ref[own, 2 * p * LANES:(2 * p + 1) * LANES],
                                  q_ref[own, (2 * p + 1) * LANES:(2 * p + 2) * LANES]], axis=0)
            s_prev = _dot_nt(qs, k_prev)
            if blk == 0:
                s_prev = s_prev + first_bias
            s = jnp.where(in_own_block, _dot_nt(qs, kc_ref[own, kv_cols]), s_prev)
            sink = jnp.where(top_rows, sink_ref[2 * p], sink_ref[2 * p + 1])
            m = jnp.maximum(jnp.max(s, axis=-1, keepdims=True), sink)
            e = jnp.exp(s - m)
            denom = jnp.sum(e, axis=-1, keepdims=True) + jnp.exp(sink - m)
            e_own = jnp.where(in_own_block, e, 0.0).astype(BF16)
            e_prev = jnp.where(in_own_block, 0.0, e).astype(BF16)
            o2 = (_dot(e_own, vc_ref[own, kv_cols]) + _dot(e_prev, v_prev)) / denom
            outs.append(jnp.where(low_half, o2[:WINDOW], o2[WINDOW:]))
        o_ref[own, :] = jnp.concatenate(outs, axis=1).astype(BF16)


def _swa(q, k2, v2, sinks, bsz, seq):
    steps_per_seq = seq // (SWA_BLOCKS * WINDOW)
    cur = lambda b, n: (b * steps_per_seq + n, 0)
    prev = lambda b, n: (SWA_BLOCKS * (b * steps_per_seq + n) - jnp.minimum(n, 1), 0)
    return pl.pallas_call(
        _swa_kernel,
        grid=(bsz, steps_per_seq),
        in_specs=[
            pl.BlockSpec(memory_space=pltpu.SMEM),
            pl.BlockSpec((SWA_BLOCKS * WINDOW, 2 * ATTN_WIDTH), cur),
            pl.BlockSpec((WINDOW, 2 * KV_WIDTH), prev),
            pl.BlockSpec((SWA_BLOCKS * WINDOW, 2 * KV_WIDTH), cur),
            pl.BlockSpec((WINDOW, 2 * KV_WIDTH), prev),
            pl.BlockSpec((SWA_BLOCKS * WINDOW, 2 * KV_WIDTH), cur),
        ],
        out_specs=pl.BlockSpec((SWA_BLOCKS * WINDOW, ATTN_WIDTH), cur),
        out_shape=jax.ShapeDtypeStruct((bsz * seq, ATTN_WIDTH), BF16),
        compiler_params=_params("parallel", "parallel"),
        name="swa",
    )(sinks, q, k2, k2, v2, v2)


def _branch_kernel(gu_ref, vn_ref, at_ref, ga_ref, gb_ref, ws_ref, bs_ref, wa_ref, wb_ref, o_ref, gated_ref):
    bm = gu_ref.shape[0]
    t_idx = lax.broadcasted_iota(jnp.int32, (CHUNK, CHUNK), 0)
    s_idx = lax.broadcasted_iota(jnp.int32, (CHUNK, CHUNK), 1)
    causal = s_idx <= t_idx
    bs = bs_ref[...]
    for g in range(GMLP_GROUPS):
        w = jnp.where(causal, ws_ref[g], 0.0).astype(BF16)
        bias = bs[:, g:g + 1]
        cols = slice(g * LANES, (g + 1) * LANES)
        for c in range(bm // CHUNK):
            rows = slice(c * CHUNK, (c + 1) * CHUNK)
            mixed = _dot(w, vn_ref[rows, cols]) + bias
            gated_ref[rows, cols] = (gu_ref[rows, cols] * mixed).astype(BF16)
    ya = _dot(gated_ref[...], wa_ref[...])
    yb = _dot(at_ref[...], wb_ref[...])
    o_ref[...] = (ga_ref[...] * ya + gb_ref[...] * yb).astype(BF16)


def _branch(gu, vn, attn, gates, w_s, b_s_t, w_a, w_b, bm):
    t = gu.shape[0]
    row = lambda i: (i, 0)
    const2 = lambda i: (0, 0)
    return pl.pallas_call(
        _branch_kernel,
        grid=(t // bm,),
        in_specs=[
            pl.BlockSpec((bm, GMLP_WIDTH), row),
            pl.BlockSpec((bm, GMLP_WIDTH), row),
            pl.BlockSpec((bm, ATTN_WIDTH), row),
            pl.BlockSpec((bm, D_MODEL), lambda i: (i, 0)),
            pl.BlockSpec((bm, D_MODEL), lambda i: (i, 1)),
            pl.BlockSpec((GMLP_GROUPS, CHUNK, CHUNK), lambda i: (0, 0, 0)),
            pl.BlockSpec((CHUNK, GMLP_GROUPS), const2),
            pl.BlockSpec((GMLP_WIDTH, D_MODEL), const2),
            pl.BlockSpec((ATTN_WIDTH, D_MODEL), const2),
        ],
        out_specs=pl.BlockSpec((bm, D_MODEL), row),
        out_shape=jax.ShapeDtypeStruct((t, D_MODEL), BF16),
        scratch_shapes=[pltpu.VMEM((bm, GMLP_WIDTH), BF16)],
        compiler_params=_params("parallel"),
        name="branch",
    )(gu, vn, attn, gates, gates, w_s, b_s_t, w_a, w_b)


def _post_norm(g_ref, b_ref, o_ref, obf_ref):
    def epilogue(z_ref, rows):
        y = _layer_norm(z_ref[rows, :], g_ref[...], b_ref[...])
        o_ref[rows, :] = y
        obf_ref[rows, :] = y.astype(BF16)
    return epilogue


def _out_proj_kernel(n_tiles, m_ref, x_ref, wo_ref, g_ref, b_ref, o_ref, obf_ref, z0_ref, z1_ref):
    _deferred_epilogue(pl.program_id(0), n_tiles, (z0_ref, z1_ref),
                       lambda: ALPHA * x_ref[...] + _dot(m_ref[...], wo_ref[...]),
                       _post_norm(g_ref, b_ref, o_ref, obf_ref))


def _out_proj(merged, x, w_o, ln_g, ln_b, bm):
    t = x.shape[0]
    n_tiles = t // bm
    const2 = lambda s: (0, 0)
    return pl.pallas_call(
        functools.partial(_out_proj_kernel, n_tiles),
        grid=(n_tiles + 1,),
        in_specs=[
            pl.BlockSpec((bm, D_MODEL), _cur_tile(n_tiles)),
            pl.BlockSpec((bm, D_MODEL), _cur_tile(n_tiles)),
            pl.BlockSpec((D_MODEL, D_MODEL), const2),
            pl.BlockSpec((1, D_MODEL), const2),
            pl.BlockSpec((1, D_MODEL), const2),
        ],
        out_specs=[pl.BlockSpec((bm, D_MODEL), _done_tile), pl.BlockSpec((bm, D_MODEL), _done_tile)],
        out_shape=[jax.ShapeDtypeStruct((t, D_MODEL), F32), jax.ShapeDtypeStruct((t, D_MODEL), BF16)],
        scratch_shapes=_acc_scratch(bm, D_MODEL),
        compiler_params=_params("arbitrary"),
        name="out_proj",
    )(merged, x, w_o, ln_g, ln_b)


def _mem_kv_kernel(m_ref, w_ref, o_ref):
    o_ref[...] = _dot(m_ref[...].astype(BF16), w_ref[...].astype(BF16)).astype(BF16)


def _mem_kv(mem2d, w_xkv, layer):
    rows = mem2d.shape[0]
    return pl.pallas_call(
        _mem_kv_kernel,
        grid=(2,),
        in_specs=[
            pl.BlockSpec((rows, D_MODEL), lambda j: (0, 0)),
            pl.BlockSpec((None, D_MODEL, X_WIDTH), lambda j: (layer, 0, j)),
        ],
        out_specs=pl.BlockSpec((rows, X_WIDTH), lambda j: (0, j)),
        out_shape=jax.ShapeDtypeStruct((rows, 2 * X_WIDTH), BF16),
        compiler_params=_params("parallel"),
        name="mem_kv",
    )(mem2d, w_xkv)


def _xattn_kernel(n_tiles, xbf_ref, x_ref, k_ref, v_ref, wq_ref, wo_ref, g_ref, b_ref, o_ref, obf_ref,
                  z0_ref, z1_ref):
    def compute_z():
        q = _dot(xbf_ref[...], wq_ref[...]).astype(BF16)
        scale = X_HEAD_DIM ** -0.5
        outs = []
        for h in range(X_HEADS):
            cols = slice(h * X_HEAD_DIM, (h + 1) * X_HEAD_DIM)
            s = _dot_nt(q[:, cols], k_ref[:, cols]) * scale
            m = jnp.max(s, axis=-1, keepdims=True)
            p = jnp.exp(s - m)
            denom = jnp.sum(p, axis=-1, keepdims=True)
            outs.append(_dot(p.astype(BF16), v_ref[:, cols]) / denom)
        o = jnp.concatenate(outs, axis=1).astype(BF16)
        return ALPHA * x_ref[...] + _dot(o, wo_ref[...])

    _deferred_epilogue(pl.program_id(0), n_tiles, (z0_ref, z1_ref), compute_z,
                       _post_norm(g_ref, b_ref, o_ref, obf_ref))


def _xattn(x_bf, x, kv, w_xq, w_xo, ln_g, ln_b, bm, seq):
    t = x.shape[0]
    n_tiles = t // bm
    tiles_per_seq = seq // bm
    batch = lambda s: jnp.minimum(s, n_tiles - 1) // tiles_per_seq
    const2 = lambda s: (0, 0)
    return pl.pallas_call(
        functools.partial(_xattn_kernel, n_tiles),
        grid=(n_tiles + 1,),
        in_specs=[
            pl.BlockSpec((bm, D_MODEL), _cur_tile(n_tiles)),
            pl.BlockSpec((bm, D_MODEL), _cur_tile(n_tiles)),
            pl.BlockSpec((MEM_LEN, X_WIDTH), lambda s: (batch(s), 0)),
            pl.BlockSpec((MEM_LEN, X_WIDTH), lambda s: (batch(s), 1)),
            pl.BlockSpec((D_MODEL, X_WIDTH), const2),
            pl.BlockSpec((X_WIDTH, D_MODEL), const2),
            pl.BlockSpec((1, D_MODEL), const2),
            pl.BlockSpec((1, D_MODEL), const2),
        ],
        out_specs=[pl.BlockSpec((bm, D_MODEL), _done_tile), pl.BlockSpec((bm, D_MODEL), _done_tile)],
        out_shape=[jax.ShapeDtypeStruct((t, D_MODEL), F32), jax.ShapeDtypeStruct((t, D_MODEL), BF16)],
        scratch_shapes=_acc_scratch(bm, D_MODEL),
        compiler_params=_params("arbitrary"),
        name="xattn",
    )(x_bf, x, kv, kv, w_xq, w_xo, ln_g, ln_b)


def _mlp_kernel(n_f, xbf_ref, x_ref, wu_ref, wd_ref, g_ref, b_ref, o_ref, obf_ref, acc_ref, h0_ref, h1_ref):
    f = pl.program_id(1)
    h_refs = (h0_ref, h1_ref)

    def up(h_ref):
        h = jnp.maximum(_dot(xbf_ref[...], wu_ref[...]), 0.0)
        h_ref[...] = (h * h).astype(BF16)

    def down(h_ref):
        acc_ref[...] += _dot(h_ref[...], wd_ref[...])

    @pl.when(f == 0)
    def _():
        acc_ref[...] = jnp.zeros_like(acc_ref)
        up(h_refs[0])

    for parity in range(2):
        @pl.when((f > 0) & (f < n_f) & (f % 2 == parity))
        def _():
            down(h_refs[1 - parity])
            up(h_refs[parity])

    @pl.when(f == n_f)
    def _():
        down(h_refs[1 - n_f % 2])
        z = ALPHA * x_ref[...] + acc_ref[...]
        y = _layer_norm(z, g_ref[...], b_ref[...])
        o_ref[...] = y
        obf_ref[...] = y.astype(BF16)


def _mlp(x_bf, x, w_up, w_down, ln_g, ln_b, bm, bf):
    t = x.shape[0]
    n_f = D_FF // bf
    row = lambda i, f: (i, 0)
    const2 = lambda i, f: (0, 0)
    return pl.pallas_call(
        functools.partial(_mlp_kernel, n_f),
        grid=(t // bm, n_f + 1),
        in_specs=[
            pl.BlockSpec((bm, D_MODEL), row),
            pl.BlockSpec((bm, D_MODEL), row),
            pl.BlockSpec((D_MODEL, bf), lambda i, f: (0, jnp.minimum(f, n_f - 1))),
            pl.BlockSpec((bf, D_MODEL), lambda i, f: (jnp.maximum(f - 1, 0), 0)),
            pl.BlockSpec((1, D_MODEL), const2),
            pl.BlockSpec((1, D_MODEL), const2),
        ],
        out_specs=[pl.BlockSpec((bm, D_MODEL), row), pl.BlockSpec((bm, D_MODEL), row)],
        out_shape=[jax.ShapeDtypeStruct((t, D_MODEL), F32), jax.ShapeDtypeStruct((t, D_MODEL), BF16)],
        scratch_shapes=[pltpu.VMEM((bm, D_MODEL), F32), pltpu.VMEM((bm, bf), BF16), pltpu.VMEM((bm, bf), BF16)],
        compiler_params=_params("parallel", "arbitrary"),
        name="mlp",
    )(x_bf, x, w_up, w_down, ln_g, ln_b)


def _rope_tables(seq):
    inv = 1.0 / (ROPE_THETA ** (jnp.arange(0, HEAD_DIM, 2, dtype=F32) / HEAD_DIM))
    ang = jnp.arange(seq, dtype=F32)[:, None] * inv[None, :]
    cos, sin = jnp.cos(ang), jnp.sin(ang)
    reps = LANES // HEAD_DIM
    cos_t = jnp.tile(jnp.concatenate([cos, cos], axis=1), (1, reps))
    sin_t = jnp.tile(jnp.concatenate([-sin, sin], axis=1), (1, reps))
    return cos_t, sin_t


def kernel(x, mem, w_in, b_gate, ln_v_g, ln_v_b, w_s, b_s, sinks, w_br_a, w_br_b, w_o, ln1_g, ln1_b,
           w_xq, w_xkv, w_xo, ln2_g, ln2_b, w_up, w_down, ln3_g, ln3_b):
    bsz, seq, d = x.shape
    t = bsz * seq
    cos_t, sin_t = _rope_tables(seq)
    xf = x.reshape(t, d)
    xb = xf.astype(BF16)
    mem2d = mem.reshape(bsz * MEM_LEN, d)
    row = lambda a: a.reshape(1, -1)
    for l in range(DEPTH):
        gu = _proj_u(xb, w_in, l, bm=1024)
        vn = _proj_v(xb, w_in, l, row(ln_v_g[l]), row(ln_v_b[l]), bm=1024)
        q = _proj_q(xb, w_in, l, cos_t, sin_t, bm=1024, seq=seq)
        k2, v2 = _proj_kv(xb, w_in, l, cos_t, sin_t, bm=1024, seq=seq)
        gates = _proj_gate(xb, w_in, l, row(b_gate[l]), bm=1024, bn=1024)
        attn = _swa(q, k2, v2, sinks[l], bsz, seq)
        merged = _branch(gu, vn, attn, gates, w_s[l], b_s[l].T, w_br_a[l].astype(BF16),
                         w_br_b[l].astype(BF16), bm=512)
        xf, xb = _out_proj(merged, xf, w_o[l].astype(BF16), row(ln1_g[l]), row(ln1_b[l]), bm=512)
        kv = _mem_kv(mem2d, w_xkv, l)
        xf, xb = _xattn(xb, xf, kv, w_xq[l].astype(BF16), w_xo[l].astype(BF16), row(ln2_g[l]),
                        row(ln2_b[l]), bm=512, seq=seq)
        xf, xb = _mlp(xb, xf, w_up[l].astype(BF16), w_down[l].astype(BF16), row(ln3_g[l]), row(ln3_b[l]),
                      bm=512, bf=1024)
    return xf.reshape(bsz, seq, d)
```

```python
import functools

import jax
import jax.numpy as jnp
from jax import lax
from jax.experimental import pallas as pl
from jax.experimental.pallas import tpu as pltpu

D_MODEL = 2048
DEPTH = 2
MEM_LEN = 256
CHUNK = 128
GMLP_WIDTH = 1024
GMLP_GROUPS = 8
N_Q_HEADS = 16
N_KV_HEADS = 4
HEAD_DIM = 64
ATTN_WIDTH = N_Q_HEADS * HEAD_DIM
KV_WIDTH = N_KV_HEADS * HEAD_DIM
WINDOW = 128
ROPE_THETA = 10000.0
X_HEADS = 4
X_HEAD_DIM = 128
X_WIDTH = X_HEADS * X_HEAD_DIM
D_FF = 4 * D_MODEL
LN_EPS = 1e-5
ALPHA = (2 * DEPTH) ** 0.25

OFF_V = GMLP_WIDTH
OFF_Q = 2 * GMLP_WIDTH
OFF_K = OFF_Q + ATTN_WIDTH
OFF_GATE = OFF_K + 2 * KV_WIDTH
GATE_WIDTH = 2 * D_MODEL

LANES = 128
VMEM_LIMIT_BYTES = 56 * 1024 * 1024
SWA_BLOCKS = 2
EPILOGUE_ROWS = 32
SCORE_SCALE = HEAD_DIM ** -0.5

BF16 = jnp.bfloat16
F32 = jnp.float32


def _params(*semantics):
    return pltpu.CompilerParams(dimension_semantics=semantics, vmem_limit_bytes=VMEM_LIMIT_BYTES)


def _dot(a, b):
    return jnp.dot(a, b, preferred_element_type=F32)


def _dot_nt(a, b):
    return lax.dot_general(a, b, (((1,), (1,)), ((), ())), preferred_element_type=F32)


def _layer_norm(z, g, b):
    mu = jnp.mean(z, axis=-1, keepdims=True)
    zc = z - mu
    var = jnp.mean(zc * zc, axis=-1, keepdims=True)
    return zc * lax.rsqrt(var + LN_EPS) * g + b


def _gelu(x):
    return jax.nn.gelu(x, approximate=True)


def _cast_weight_once(w_ref, wbf_ref, step):
    @pl.when(step == 0)
    def _():
        wbf_ref[...] = w_ref[...].astype(BF16)


def _deferred_epilogue(step, n_tiles, acc_refs, compute, epilogue):
    def run_epilogue(acc_ref):
        for r in range(0, acc_ref.shape[0], EPILOGUE_ROWS):
            epilogue(acc_ref, slice(r, r + EPILOGUE_ROWS))

    @pl.when(step == 0)
    def _():
        acc_refs[0][...] = compute()

    for parity in range(2):
        @pl.when((step > 0) & (step < n_tiles) & (step % 2 == parity))
        def _():
            run_epilogue(acc_refs[1 - parity])
            acc_refs[parity][...] = compute()

    @pl.when(step == n_tiles)
    def _():
        run_epilogue(acc_refs[1 - n_tiles % 2])


def _cur_tile(n_tiles):
    return lambda s: (jnp.minimum(s, n_tiles - 1), 0)


def _done_tile(s):
    return (jnp.maximum(s - 1, 0), 0)


def _acc_scratch(bm, width):
    return [pltpu.VMEM((bm, width), F32), pltpu.VMEM((bm, width), F32)]


def _w_in_spec(layer, width, col_block):
    return pl.BlockSpec((None, D_MODEL, width), lambda s: (layer, 0, col_block))


def _proj_u_kernel(x_ref, w_ref, gu_ref, wbf_ref):
    _cast_weight_once(w_ref, wbf_ref, pl.program_id(0))
    gu_ref[...] = _gelu(_dot(x_ref[...], wbf_ref[...]))


def _proj_v_kernel(x_ref, w_ref, g_ref, b_ref, vn_ref, wbf_ref):
    _cast_weight_once(w_ref, wbf_ref, pl.program_id(0))
    v = _gelu(_dot(x_ref[...], wbf_ref[...]))
    vn_ref[...] = _layer_norm(v, g_ref[...], b_ref[...]).astype(BF16)


def _proj_u_from_f32_kernel(x_ref, w_ref, gu_ref, xb_ref, wbf_ref):
    _cast_weight_once(w_ref, wbf_ref, pl.program_id(0))
    xb = x_ref[...].astype(BF16)
    xb_ref[...] = xb
    gu_ref[...] = _gelu(_dot(xb, wbf_ref[...]))


def _proj_u(x_bf, w_in, layer, bm):
    t = x_bf.shape[0]
    return pl.pallas_call(
        _proj_u_kernel,
        grid=(t // bm,),
        in_specs=[
            pl.BlockSpec((bm, D_MODEL), lambda i: (i, 0)),
            _w_in_spec(layer, GMLP_WIDTH, 0),
        ],
        out_specs=pl.BlockSpec((bm, GMLP_WIDTH), lambda i: (i, 0)),
        out_shape=jax.ShapeDtypeStruct((t, GMLP_WIDTH), F32),
        scratch_shapes=[pltpu.VMEM((D_MODEL, GMLP_WIDTH), BF16)],
        compiler_params=_params("arbitrary"),
        name="proj_u",
    )(x_bf, w_in)


def _proj_u_from_f32(x, w_in, layer, bm):
    t = x.shape[0]
    return pl.pallas_call(
        _proj_u_from_f32_kernel,
        grid=(t // bm,),
        in_specs=[
            pl.BlockSpec((bm, D_MODEL), lambda i: (i, 0)),
            _w_in_spec(layer, GMLP_WIDTH, 0),
        ],
        out_specs=[pl.BlockSpec((bm, GMLP_WIDTH), lambda i: (i, 0)), pl.BlockSpec((bm, D_MODEL), lambda i: (i, 0))],
        out_shape=[jax.ShapeDtypeStruct((t, GMLP_WIDTH), F32), jax.ShapeDtypeStruct((t, D_MODEL), BF16)],
        scratch_shapes=[pltpu.VMEM((D_MODEL, GMLP_WIDTH), BF16)],
        compiler_params=_params("arbitrary"),
        name="proj_u_f32",
    )(x, w_in)


def _proj_v(x_bf, w_in, layer, ln_g, ln_b, bm):
    t = x_bf.shape[0]
    return pl.pallas_call(
        _proj_v_kernel,
        grid=(t // bm,),
        in_specs=[
            pl.BlockSpec((bm, D_MODEL), lambda i: (i, 0)),
            _w_in_spec(layer, GMLP_WIDTH, OFF_V // GMLP_WIDTH),
            pl.BlockSpec((1, GMLP_WIDTH), lambda i: (0, 0)),
            pl.BlockSpec((1, GMLP_WIDTH), lambda i: (0, 0)),
        ],
        out_specs=pl.BlockSpec((bm, GMLP_WIDTH), lambda i: (i, 0)),
        out_shape=jax.ShapeDtypeStruct((t, GMLP_WIDTH), BF16),
        scratch_shapes=[pltpu.VMEM((D_MODEL, GMLP_WIDTH), BF16)],
        compiler_params=_params("arbitrary"),
        name="proj_v",
    )(x_bf, w_in, ln_g, ln_b)


def _rope(x, cos, sin_signed, first_half):
    outs = []
    for c in range(x.shape[1] // LANES):
        xs = x[:, c * LANES:(c + 1) * LANES]
        swapped = jnp.where(first_half, pltpu.roll(xs, LANES - HEAD_DIM // 2, 1),
                            pltpu.roll(xs, HEAD_DIM // 2, 1))
        outs.append(xs * cos + swapped * sin_signed)
    return outs


def _duplicate_heads(cols, low_half):
    outs = []
    for xs in cols:
        other = pltpu.roll(xs, HEAD_DIM, 1)
        outs.append(jnp.where(low_half, xs, other))
        outs.append(jnp.where(low_half, other, xs))
    return outs


def _head_masks(shape):
    lane = lax.broadcasted_iota(jnp.int32, shape, 1)
    return (lane % HEAD_DIM) < (HEAD_DIM // 2), (lane % LANES) < HEAD_DIM


def _proj_q_kernel(n_tiles, x_ref, w_ref, cos_ref, sin_ref, q_ref, wbf_ref, acc0_ref, acc1_ref):
    step = pl.program_id(0)
    _cast_weight_once(w_ref, wbf_ref, step)

    def epilogue(acc_ref, rows):
        first_half, low_half = _head_masks((EPILOGUE_ROWS, LANES))
        cols = _rope(acc_ref[rows, :], cos_ref[rows, :] * SCORE_SCALE, sin_ref[rows, :] * SCORE_SCALE, first_half)
        padded = []
        for c in cols:
            padded.append(jnp.where(low_half, c, 0.0))
            padded.append(jnp.where(low_half, 0.0, c))
        q_ref[rows, :] = jnp.concatenate(padded, axis=1).astype(BF16)

    _deferred_epilogue(step, n_tiles, (acc0_ref, acc1_ref), lambda: _dot(x_ref[...], wbf_ref[...]), epilogue)


def _proj_kv_kernel(n_tiles, x_ref, w_ref, cos_ref, sin_ref, k_ref, v_ref, wbf_ref, acc0_ref, acc1_ref):
    step = pl.program_id(0)
    _cast_weight_once(w_ref, wbf_ref, step)

    def epilogue(acc_ref, rows):
        first_half, low_half = _head_masks((EPILOGUE_ROWS, LANES))
        acc = acc_ref[rows, :]
        k_cols = _rope(acc[:, :KV_WIDTH], cos_ref[rows, :], sin_ref[rows, :], first_half)
        v_cols = [acc[:, KV_WIDTH + c * LANES:KV_WIDTH + (c + 1) * LANES] for c in range(KV_WIDTH // LANES)]
        k_ref[rows, :] = jnp.concatenate(_duplicate_heads(k_cols, low_half), axis=1).astype(BF16)
        v_ref[rows, :] = jnp.concatenate(_duplicate_heads(v_cols, low_half), axis=1).astype(BF16)

    _deferred_epilogue(step, n_tiles, (acc0_ref, acc1_ref), lambda: _dot(x_ref[...], wbf_ref[...]), epilogue)


def _rope_table_spec(bm, tiles_per_seq):
    return pl.BlockSpec((bm, LANES), lambda s: (jnp.maximum(s - 1, 0) % tiles_per_seq, 0))


def _proj_q(x_bf, w_in, layer, cos_t, sin_t, bm, seq):
    t = x_bf.shape[0]
    n_tiles = t // bm
    return pl.pallas_call(
        functools.partial(_proj_q_kernel, n_tiles),
        grid=(n_tiles + 1,),
        in_specs=[
            pl.BlockSpec((bm, D_MODEL), _cur_tile(n_tiles)),
            _w_in_spec(layer, ATTN_WIDTH, OFF_Q // ATTN_WIDTH),
            _rope_table_spec(bm, seq // bm),
            _rope_table_spec(bm, seq // bm),
        ],
        out_specs=pl.BlockSpec((bm, 2 * ATTN_WIDTH), _done_tile),
        out_shape=jax.ShapeDtypeStruct((t, 2 * ATTN_WIDTH), BF16),
        scratch_shapes=[pltpu.VMEM((D_MODEL, ATTN_WIDTH), BF16)] + _acc_scratch(bm, ATTN_WIDTH),
        compiler_params=_params("arbitrary"),
        name="proj_q",
    )(x_bf, w_in, cos_t, sin_t)


def _proj_kv(x_bf, w_in, layer, cos_t, sin_t, bm, seq):
    t = x_bf.shape[0]
    n_tiles = t // bm
    return pl.pallas_call(
        functools.partial(_proj_kv_kernel, n_tiles),
        grid=(n_tiles + 1,),
        in_specs=[
            pl.BlockSpec((bm, D_MODEL), _cur_tile(n_tiles)),
            _w_in_spec(layer, 2 * KV_WIDTH, OFF_K // (2 * KV_WIDTH)),
            _rope_table_spec(bm, seq // bm),
            _rope_table_spec(bm, seq // bm),
        ],
        out_specs=[
            pl.BlockSpec((bm, 2 * KV_WIDTH), _done_tile),
            pl.BlockSpec((bm, 2 * KV_WIDTH), _done_tile),
        ],
        out_shape=[
            jax.ShapeDtypeStruct((t, 2 * KV_WIDTH), BF16),
            jax.ShapeDtypeStruct((t, 2 * KV_WIDTH), BF16),
        ],
        scratch_shapes=[pltpu.VMEM((D_MODEL, 2 * KV_WIDTH), BF16)] + _acc_scratch(bm, 2 * KV_WIDTH),
        compiler_params=_params("arbitrary"),
        name="proj_kv",
    )(x_bf, w_in, cos_t, sin_t)


def _proj_gate_kernel(n_tiles, m_tiles, n_side, x_ref, w0_ref, w1_ref, b_ref, *refs):
    side_in, o_ref, side_out = refs[:n_side], refs[n_side], refs[n_side + 1:2 * n_side + 1]
    wbf_ref, acc0_ref, acc1_ref = refs[2 * n_side + 1:]
    step = pl.program_id(0)
    half = w0_ref.shape[1]

    @pl.when((step % m_tiles == 0) & (step < n_tiles))
    def _():
        wbf_ref[:, :half] = w0_ref[...].astype(BF16)
        wbf_ref[:, half:] = w1_ref[...].astype(BF16)

    def compute():
        for src, dst in zip(side_in, side_out):
            dst[...] = src[...].astype(BF16)
        return _dot(x_ref[...], wbf_ref[...])

    def epilogue(acc_ref, rows):
        o_ref[rows, :] = jax.nn.sigmoid(acc_ref[rows, :] + b_ref[...])

    _deferred_epilogue(step, n_tiles, (acc0_ref, acc1_ref), compute, epilogue)


def _proj_gate(x_bf, w_in, layer, b_gate, side_weights, bm, bn):
    t = x_bf.shape[0]
    m_tiles = t // bm
    n_tiles = m_tiles * (GATE_WIDTH // bn)
    half = bn // 2
    first = OFF_GATE // half
    cur = lambda s: jnp.minimum(s, n_tiles - 1)
    done = lambda s: jnp.maximum(s - 1, 0)
    slab = lambda w: (w.shape[1] // n_tiles, w.shape[2])
    return pl.pallas_call(
        functools.partial(_proj_gate_kernel, n_tiles, m_tiles, len(side_weights)),
        grid=(n_tiles + 1,),
        in_specs=[
            pl.BlockSpec((bm, D_MODEL), lambda s: (cur(s) % m_tiles, 0)),
            pl.BlockSpec((None, D_MODEL, half), lambda s: (layer, 0, first + 2 * (cur(s) // m_tiles))),
            pl.BlockSpec((None, D_MODEL, half), lambda s: (layer, 0, first + 2 * (cur(s) // m_tiles) + 1)),
            pl.BlockSpec((1, bn), lambda s: (0, done(s) // m_tiles)),
        ] + [pl.BlockSpec((None,) + slab(w), lambda s: (layer, cur(s), 0)) for w in side_weights],
        out_specs=[pl.BlockSpec((bm, bn), lambda s: (done(s) % m_tiles, done(s) // m_tiles))]
        + [pl.BlockSpec(slab(w), lambda s: (cur(s), 0)) for w in side_weights],
        out_shape=[jax.ShapeDtypeStruct((t, GATE_WIDTH), F32)]
        + [jax.ShapeDtypeStruct(w.shape[1:], BF16) for w in side_weights],
        scratch_shapes=[pltpu.VMEM((D_MODEL, bn), BF16)] + _acc_scratch(bm, bn),
        compiler_params=_params("arbitrary"),
        name="proj_gate",
    )(x_bf, w_in, w_in, b_gate, *side_weights)


def _swa_kernel(sink_ref, q_ref, kp_ref, kc_ref, vp_ref, vc_ref, o_ref):
    n = pl.program_id(1)
    pair_rows = 2 * WINDOW
    lane = lax.broadcasted_iota(jnp.int32, (WINDOW, LANES), 1)
    low_half = lane < HEAD_DIM
    row = lax.broadcasted_iota(jnp.int32, (pair_rows, WINDOW), 0)
    key = lax.broadcasted_iota(jnp.int32, (pair_rows, WINDOW), 1)
    in_own_block = key <= (row % WINDOW)
    top_rows = lax.broadcasted_iota(jnp.int32, (pair_rows, 1), 0) < WINDOW
    first_key = jnp.where(n > 0, 0, WINDOW)
    first_bias = jnp.where(lax.broadcasted_iota(jnp.int32, (1, WINDOW), 1) >= first_key, 0.0, -jnp.inf)
    for blk in range(SWA_BLOCKS):
        own = slice(blk * WINDOW, (blk + 1) * WINDOW)
        outs = []
        for p in range(N_Q_HEADS // 2):
            g = (2 * p) // (N_Q_HEADS // N_KV_HEADS)
            kv_cols = slice(g * LANES, (g + 1) * LANES)
            if blk == 0:
                k_prev, v_prev = kp_ref[:, kv_cols], vp_ref[:, kv_cols]
            else:
                before = slice((blk - 1) * WINDOW, blk * WINDOW)
                k_prev, v_prev = kc_ref[before, kv_cols], vc_ref[before, kv_cols]
            qs = jnp.concatenate([q_ref[own, 2 * p * LANES:(2 * p + 1) * LANES],
                                  q_ref[own, (2 * p + 1) * LANES:(2 * p + 2) * LANES]], axis=0)
            s_prev = _dot_nt(qs, k_prev)
            if blk == 0:
                s_prev = s_prev + first_bias
            s = jnp.where(in_own_block, _dot_nt(qs, kc_ref[own, kv_cols]), s_prev)
            sink = jnp.where(top_rows, sink_ref[2 * p], sink_ref[2 * p + 1])
            m = jnp.maximum(jnp.max(s, axis=-1, keepdims=True), sink)
            e = jnp.exp(s - m)
            denom = jnp.sum(e, axis=-1, keepdims=True) + jnp.exp(sink - m)
            e_own = jnp.where(in_own_block, e, 0.0).astype(BF16)
            e_prev = jnp.where(in_own_block, 0.0, e).astype(BF16)
            o2 = (_dot(e_own, vc_ref[own, kv_cols]) + _dot(e_prev, v_prev)) / denom
            outs.append(jnp.where(low_half, o2[:WINDOW], o2[WINDOW:]))
        o_ref[own, :] = jnp.concatenate(outs, axis=1).astype(BF16)


def _swa(q, k2, v2, sinks, bsz, seq):
    steps_per_seq = seq // (SWA_BLOCKS * WINDOW)
    cur = lambda b, n: (b * steps_per_seq + n, 0)
    prev = lambda b, n: (SWA_BLOCKS * (b * steps_per_seq + n) - jnp.minimum(n, 1), 0)
    return pl.pallas_call(
        _swa_kernel,
        grid=(bsz, steps_per_seq),
        in_specs=[
            pl.BlockSpec(memory_space=pltpu.SMEM),
            pl.BlockSpec((SWA_BLOCKS * WINDOW, 2 * ATTN_WIDTH), cur),
            pl.BlockSpec((WINDOW, 2 * KV_WIDTH), prev),
            pl.BlockSpec((SWA_BLOCKS * WINDOW, 2 * KV_WIDTH), cur),
            pl.BlockSpec((WINDOW, 2 * KV_WIDTH), prev),
            pl.BlockSpec((SWA_BLOCKS * WINDOW, 2 * KV_WIDTH), cur),
        ],
        out_specs=pl.BlockSpec((SWA_BLOCKS * WINDOW, ATTN_WIDTH), cur),
        out_shape=jax.ShapeDtypeStruct((bsz * seq, ATTN_WIDTH), BF16),
        compiler_params=_params("parallel", "parallel"),
        name="swa",
    )(sinks, q, k2, k2, v2, v2)


def _branch_kernel(gu_ref, vn_ref, at_ref, ga_ref, gb_ref, ws_ref, bs_ref, wa_ref, wb_ref, o_ref, gated_ref):
    bm = gu_ref.shape[0]
    t_idx = lax.broadcasted_iota(jnp.int32, (CHUNK, CHUNK), 0)
    s_idx = lax.broadcasted_iota(jnp.int32, (CHUNK, CHUNK), 1)
    causal = s_idx <= t_idx
    bs = bs_ref[...]
    for g in range(GMLP_GROUPS):
        w = jnp.where(causal, ws_ref[g], 0.0).astype(BF16)
        bias = bs[:, g:g + 1]
        cols = slice(g * LANES, (g + 1) * LANES)
        for c in range(bm // CHUNK):
            rows = slice(c * CHUNK, (c + 1) * CHUNK)
            mixed = _dot(w, vn_ref[rows, cols]) + bias
            gated_ref[rows, cols] = (gu_ref[rows, cols] * mixed).astype(BF16)
    ya = _dot(gated_ref[...], wa_ref[...])
    yb = _dot(at_ref[...], wb_ref[...])
    o_ref[...] = (ga_ref[...] * ya + gb_ref[...] * yb).astype(BF16)


def _branch(gu, vn, attn, gates, w_s, b_s_t, w_a, w_b, bm):
    t = gu.shape[0]
    row = lambda i: (i, 0)
    const2 = lambda i: (0, 0)
    return pl.pallas_call(
        _branch_kernel,
        grid=(t // bm,),
        in_specs=[
            pl.BlockSpec((bm, GMLP_WIDTH), row),
            pl.BlockSpec((bm, GMLP_WIDTH), row),
            pl.BlockSpec((bm, ATTN_WIDTH), row),
            pl.BlockSpec((bm, D_MODEL), lambda i: (i, 0)),
            pl.BlockSpec((bm, D_MODEL), lambda i: (i, 1)),
            pl.BlockSpec((GMLP_GROUPS, CHUNK, CHUNK), lambda i: (0, 0, 0)),
            pl.BlockSpec((CHUNK, GMLP_GROUPS), const2),
            pl.BlockSpec((GMLP_WIDTH, D_MODEL), const2),
            pl.BlockSpec((ATTN_WIDTH, D_MODEL), const2),
        ],
        out_specs=pl.BlockSpec((bm, D_MODEL), row),
        out_shape=jax.ShapeDtypeStruct((t, D_MODEL), BF16),
        scratch_shapes=[pltpu.VMEM((bm, GMLP_WIDTH), BF16)],
        compiler_params=_params("parallel"),
        name="branch",
    )(gu, vn, attn, gates, gates, w_s, b_s_t, w_a, w_b)


def _post_norm(g_ref, b_ref, o_ref, obf_ref):
    def epilogue(z_ref, rows):
        y = _layer_norm(z_ref[rows, :], g_ref[...], b_ref[...])
        o_ref[rows, :] = y
        obf_ref[rows, :] = y.astype(BF16)
    return epilogue


def _out_proj_kernel(n_tiles, m_ref, x_ref, wo_ref, g_ref, b_ref, o_ref, obf_ref, z0_ref, z1_ref):
    _deferred_epilogue(pl.program_id(0), n_tiles, (z0_ref, z1_ref),
                       lambda: ALPHA * x_ref[...] + _dot(m_ref[...], wo_ref[...]),
                       _post_norm(g_ref, b_ref, o_ref, obf_ref))


def _out_proj(merged, x, w_o, ln_g, ln_b, bm):
    t = x.shape[0]
    n_tiles = t // bm
    const2 = lambda s: (0, 0)
    return pl.pallas_call(
        functools.partial(_out_proj_kernel, n_tiles),
        grid=(n_tiles + 1,),
        in_specs=[
            pl.BlockSpec((bm, D_MODEL), _cur_tile(n_tiles)),
            pl.BlockSpec((bm, D_MODEL), _cur_tile(n_tiles)),
            pl.BlockSpec((D_MODEL, D_MODEL), const2),
            pl.BlockSpec((1, D_MODEL), const2),
            pl.BlockSpec((1, D_MODEL), const2),
        ],
        out_specs=[pl.BlockSpec((bm, D_MODEL), _done_tile), pl.BlockSpec((bm, D_MODEL), _done_tile)],
        out_shape=[jax.ShapeDtypeStruct((t, D_MODEL), F32), jax.ShapeDtypeStruct((t, D_MODEL), BF16)],
        scratch_shapes=_acc_scratch(bm, D_MODEL),
        compiler_params=_params("arbitrary"),
        name="out_proj",
    )(merged, x, w_o, ln_g, ln_b)


def _mem_kv_kernel(m_ref, w_ref, o_ref):
    o_ref[...] = _dot(m_ref[...].astype(BF16), w_ref[...].astype(BF16)).astype(BF16)


def _mem_kv(mem2d, w_xkv, layer):
    rows = mem2d.shape[0]
    return pl.pallas_call(
        _mem_kv_kernel,
        grid=(2,),
        in_specs=[
            pl.BlockSpec((rows, D_MODEL), lambda j: (0, 0)),
            pl.BlockSpec((None, D_MODEL, X_WIDTH), lambda j: (layer, 0, j)),
        ],
        out_specs=pl.BlockSpec((rows, X_WIDTH), lambda j: (0, j)),
        out_shape=jax.ShapeDtypeStruct((rows, 2 * X_WIDTH), BF16),
        compiler_params=_params("parallel"),
        name="mem_kv",
    )(mem2d, w_xkv)


def _xattn_kernel(n_tiles, xbf_ref, x_ref, k_ref, v_ref, wq_ref, wo_ref, g_ref, b_ref, o_ref, obf_ref,
                  z0_ref, z1_ref):
    def compute_z():
        q = _dot(xbf_ref[...], wq_ref[...]).astype(BF16)
        scale = X_HEAD_DIM ** -0.5
        outs = []
        for h in range(X_HEADS):
            cols = slice(h * X_HEAD_DIM, (h + 1) * X_HEAD_DIM)
            s = _dot_nt(q[:, cols], k_ref[:, cols]) * scale
            m = jnp.max(s, axis=-1, keepdims=True)
            p = jnp.exp(s - m)
            denom = jnp.sum(p, axis=-1, keepdims=True)
            outs.append(_dot(p.astype(BF16), v_ref[:, cols]) / denom)
        o = jnp.concatenate(outs, axis=1).astype(BF16)
        return ALPHA * x_ref[...] + _dot(o, wo_ref[...])

    _deferred_epilogue(pl.program_id(0), n_tiles, (z0_ref, z1_ref), compute_z,
                       _post_norm(g_ref, b_ref, o_ref, obf_ref))


def _xattn(x_bf, x, kv, w_xq, w_xo, ln_g, ln_b, bm, seq):
    t = x.shape[0]
    n_tiles = t // bm
    tiles_per_seq = seq // bm
    batch = lambda s: jnp.minimum(s, n_tiles - 1) // tiles_per_seq
    const2 = lambda s: (0, 0)
    return pl.pallas_call(
        functools.partial(_xattn_kernel, n_tiles),
        grid=(n_tiles + 1,),
        in_specs=[
            pl.BlockSpec((bm, D_MODEL), _cur_tile(n_tiles)),
            pl.BlockSpec((bm, D_MODEL), _cur_tile(n_tiles)),
            pl.BlockSpec((MEM_LEN, X_WIDTH), lambda s: (batch(s), 0)),
            pl.BlockSpec((MEM_LEN, X_WIDTH), lambda s: (batch(s), 1)),
            pl.BlockSpec((D_MODEL, X_WIDTH), const2),
            pl.BlockSpec((X_WIDTH, D_MODEL), const2),
            pl.BlockSpec((1, D_MODEL), const2),
            pl.BlockSpec((1, D_MODEL), const2),
        ],
        out_specs=[pl.BlockSpec((bm, D_MODEL), _done_tile), pl.BlockSpec((bm, D_MODEL), _done_tile)],
        out_shape=[jax.ShapeDtypeStruct((t, D_MODEL), F32), jax.ShapeDtypeStruct((t, D_MODEL), BF16)],
        scratch_shapes=_acc_scratch(bm, D_MODEL),
        compiler_params=_params("arbitrary"),
        name="xattn",
    )(x_bf, x, kv, kv, w_xq, w_xo, ln_g, ln_b)


def _mlp_kernel(xbf_ref, x_ref, wu_ref, wd_ref, g_ref, b_ref, o_ref, obf_ref, acc_ref):
    f = pl.program_id(1)

    @pl.when(f == 0)
    def _():
        acc_ref[...] = jnp.zeros_like(acc_ref)

    h = jnp.maximum(_dot(xbf_ref[...], wu_ref[...]), 0.0)
    acc_ref[...] += _dot((h * h).astype(BF16), wd_ref[...])

    @pl.when(f == pl.num_programs(1) - 1)
    def _():
        z = ALPHA * x_ref[...] + acc_ref[...]
        y = _layer_norm(z, g_ref[...], b_ref[...])
        o_ref[...] = y
        obf_ref[...] = y.astype(BF16)


def _mlp(x_bf, x, w_up, w_down, ln_g, ln_b, bm, bf):
    t = x.shape[0]
    row = lambda i, f: (i, 0)
    const2 = lambda i, f: (0, 0)
    return pl.pallas_call(
        _mlp_kernel,
        grid=(t // bm, D_FF // bf),
        in_specs=[
            pl.BlockSpec((bm, D_MODEL), row),
            pl.BlockSpec((bm, D_MODEL), row),
            pl.BlockSpec((D_MODEL, bf), lambda i, f: (0, f)),
            pl.BlockSpec((bf, D_MODEL), lambda i, f: (f, 0)),
            pl.BlockSpec((1, D_MODEL), const2),
            pl.BlockSpec((1, D_MODEL), const2),
        ],
        out_specs=[pl.BlockSpec((bm, D_MODEL), row), pl.BlockSpec((bm, D_MODEL), row)],
        out_shape=[jax.ShapeDtypeStruct((t, D_MODEL), F32), jax.ShapeDtypeStruct((t, D_MODEL), BF16)],
        scratch_shapes=[pltpu.VMEM((bm, D_MODEL), F32)],
        compiler_params=_params("parallel", "arbitrary"),
        name="mlp",
    )(x_bf, x, w_up, w_down, ln_g, ln_b)


def _rope_tables(seq):
    inv = 1.0 / (ROPE_THETA ** (jnp.arange(0, HEAD_DIM, 2, dtype=F32) / HEAD_DIM))
    ang = jnp.arange(seq, dtype=F32)[:, None] * inv[None, :]
    cos, sin = jnp.cos(ang), jnp.sin(ang)
    reps = LANES // HEAD_DIM
    cos_t = jnp.tile(jnp.concatenate([cos, cos], axis=1), (1, reps))
    sin_t = jnp.tile(jnp.concatenate([-sin, sin], axis=1), (1, reps))
    return cos_t, sin_t


def kernel(x, mem, w_in, b_gate, ln_v_g, ln_v_b, w_s, b_s, sinks, w_br_a, w_br_b, w_o, ln1_g, ln1_b,
           w_xq, w_xkv, w_xo, ln2_g, ln2_b, w_up, w_down, ln3_g, ln3_b):
    bsz, seq, d = x.shape
    t = bsz * seq
    cos_t, sin_t = _rope_tables(seq)
    xf = x.reshape(t, d)
    mem2d = mem.reshape(bsz * MEM_LEN, d)
    row = lambda a: a.reshape(1, -1)
    for l in range(DEPTH):
        if l == 0:
            gu, xb = _proj_u_from_f32(xf, w_in, l, bm=512)
        else:
            gu = _proj_u(xb, w_in, l, bm=1024)
        vn = _proj_v(xb, w_in, l, row(ln_v_g[l]), row(ln_v_b[l]), bm=1024)
        q = _proj_q(xb, w_in, l, cos_t, sin_t, bm=1024, seq=seq)
        k2, v2 = _proj_kv(xb, w_in, l, cos_t, sin_t, bm=1024, seq=seq)
        gates, w_a, w_b, w_o_bf, w_up_bf, w_down_bf = _proj_gate(
            xb, w_in, l, row(b_gate[l]), [w_br_a, w_br_b, w_o, w_up, w_down], bm=1024, bn=1024)
        attn = _swa(q, k2, v2, sinks[l], bsz, seq)
        merged = _branch(gu, vn, attn, gates, w_s[l], b_s[l].T, w_a, w_b, bm=512)
        xf, xb = _out_proj(merged, xf, w_o_bf, row(ln1_g[l]), row(ln1_b[l]), bm=512)
        kv = _mem_kv(mem2d, w_xkv, l)
        xf, xb = _xattn(xb, xf, kv, w_xq[l].astype(BF16), w_xo[l].astype(BF16), row(ln2_g[l]),
                        row(ln2_b[l]), bm=512, seq=seq)
        xf, xb = _mlp(xb, xf, w_up_bf, w_down_bf, row(ln3_g[l]), row(ln3_b[l]), bm=512, bf=1024)
    return xf.reshape(bsz, seq, d)
```

```python
import functools

import jax
import jax.numpy as jnp
from jax import lax
from jax.experimental import pallas as pl
from jax.experimental.pallas import tpu as pltpu

D_MODEL = 2048
DEPTH = 2
MEM_LEN = 256
CHUNK = 128
GMLP_WIDTH = 1024
GMLP_GROUPS = 8
N_Q_HEADS = 16
N_KV_HEADS = 4
HEAD_DIM = 64
ATTN_WIDTH = N_Q_HEADS * HEAD_DIM
KV_WIDTH = N_KV_HEADS * HEAD_DIM
WINDOW = 128
ROPE_THETA = 10000.0
X_HEADS = 4
X_HEAD_DIM = 128
X_WIDTH = X_HEADS * X_HEAD_DIM
D_FF = 4 * D_MODEL
LN_EPS = 1e-5
ALPHA = (2 * DEPTH) ** 0.25

OFF_V = GMLP_WIDTH
OFF_Q = 2 * GMLP_WIDTH
OFF_K = OFF_Q + ATTN_WIDTH
OFF_GATE = OFF_K + 2 * KV_WIDTH
GATE_WIDTH = 2 * D_MODEL

LANES = 128
VMEM_LIMIT_BYTES = 56 * 1024 * 1024
SWA_BLOCKS = 4
EPILOGUE_ROWS = 32
SCORE_SCALE = HEAD_DIM ** -0.5

BF16 = jnp.bfloat16
F32 = jnp.float32


def _params(*semantics):
    return pltpu.CompilerParams(dimension_semantics=semantics, vmem_limit_bytes=VMEM_LIMIT_BYTES)


def _dot(a, b):
    return jnp.dot(a, b, preferred_element_type=F32)


def _dot_nt(a, b):
    return lax.dot_general(a, b, (((1,), (1,)), ((), ())), preferred_element_type=F32)


def _layer_norm(z, g, b):
    mu = jnp.mean(z, axis=-1, keepdims=True)
    zc = z - mu
    var = jnp.mean(zc * zc, axis=-1, keepdims=True)
    return zc * lax.rsqrt(var + LN_EPS) * g + b


def _gelu(x):
    return jax.nn.gelu(x, approximate=True)


def _cast_weight_once(w_ref, wbf_ref, step):
    @pl.when(step == 0)
    def _():
        wbf_ref[...] = w_ref[...].astype(BF16)


def _deferred_epilogue(step, n_tiles, acc_refs, compute, epilogue):
    def run_epilogue(acc_ref):
        for r in range(0, acc_ref.shape[0], EPILOGUE_ROWS):
            epilogue(acc_ref, slice(r, r + EPILOGUE_ROWS))

    @pl.when(step == 0)
    def _():
        acc_refs[0][...] = compute()

    for parity in range(2):
        @pl.when((step > 0) & (step < n_tiles) & (step % 2 == parity))
        def _():
            run_epilogue(acc_refs[1 - parity])
            acc_refs[parity][...] = compute()

    @pl.when(step == n_tiles)
    def _():
        run_epilogue(acc_refs[1 - n_tiles % 2])


def _cur_tile(n_tiles):
    return lambda s: (jnp.minimum(s, n_tiles - 1), 0)


def _done_tile(s):
    return (jnp.maximum(s - 1, 0), 0)


def _acc_scratch(bm, width):
    return [pltpu.VMEM((bm, width), F32), pltpu.VMEM((bm, width), F32)]


def _w_in_spec(layer, width, col_block):
    return pl.BlockSpec((None, D_MODEL, width), lambda s: (layer, 0, col_block))


def _proj_u_kernel(x_ref, w_ref, gu_ref, wbf_ref):
    _cast_weight_once(w_ref, wbf_ref, pl.program_id(0))
    gu_ref[...] = _gelu(_dot(x_ref[...], wbf_ref[...]))


def _proj_v_kernel(x_ref, w_ref, g_ref, b_ref, vn_ref, wbf_ref):
    _cast_weight_once(w_ref, wbf_ref, pl.program_id(0))
    v = _gelu(_dot(x_ref[...], wbf_ref[...]))
    vn_ref[...] = _layer_norm(v, g_ref[...], b_ref[...]).astype(BF16)


def _proj_u_from_f32_kernel(x_ref, w_ref, gu_ref, xb_ref, wbf_ref):
    _cast_weight_once(w_ref, wbf_ref, pl.program_id(0))
    xb = x_ref[...].astype(BF16)
    xb_ref[...] = xb
    gu_ref[...] = _gelu(_dot(xb, wbf_ref[...]))


def _proj_u(x_bf, w_in, layer, bm):
    t = x_bf.shape[0]
    return pl.pallas_call(
        _proj_u_kernel,
        grid=(t // bm,),
        in_specs=[
            pl.BlockSpec((bm, D_MODEL), lambda i: (i, 0)),
            _w_in_spec(layer, GMLP_WIDTH, 0),
        ],
        out_specs=pl.BlockSpec((bm, GMLP_WIDTH), lambda i: (i, 0)),
        out_shape=jax.ShapeDtypeStruct((t, GMLP_WIDTH), F32),
        scratch_shapes=[pltpu.VMEM((D_MODEL, GMLP_WIDTH), BF16)],
        compiler_params=_params("arbitrary"),
        name="proj_u",
    )(x_bf, w_in)


def _proj_u_from_f32(x, w_in, layer, bm):
    t = x.shape[0]
    return pl.pallas_call(
        _proj_u_from_f32_kernel,
        grid=(t // bm,),
        in_specs=[
            pl.BlockSpec((bm, D_MODEL), lambda i: (i, 0)),
            _w_in_spec(layer, GMLP_WIDTH, 0),
        ],
        out_specs=[pl.BlockSpec((bm, GMLP_WIDTH), lambda i: (i, 0)), pl.BlockSpec((bm, D_MODEL), lambda i: (i, 0))],
        out_shape=[jax.ShapeDtypeStruct((t, GMLP_WIDTH), F32), jax.ShapeDtypeStruct((t, D_MODEL), BF16)],
        scratch_shapes=[pltpu.VMEM((D_MODEL, GMLP_WIDTH), BF16)],
        compiler_params=_params("arbitrary"),
        name="proj_u_f32",
    )(x, w_in)


def _proj_v(x_bf, w_in, layer, ln_g, ln_b, bm):
    t = x_bf.shape[0]
    return pl.pallas_call(
        _proj_v_kernel,
        grid=(t // bm,),
        in_specs=[
            pl.BlockSpec((bm, D_MODEL), lambda i: (i, 0)),
            _w_in_spec(layer, GMLP_WIDTH, OFF_V // GMLP_WIDTH),
            pl.BlockSpec((1, GMLP_WIDTH), lambda i: (0, 0)),
            pl.BlockSpec((1, GMLP_WIDTH), lambda i: (0, 0)),
        ],
        out_specs=pl.BlockSpec((bm, GMLP_WIDTH), lambda i: (i, 0)),
        out_shape=jax.ShapeDtypeStruct((t, GMLP_WIDTH), BF16),
        scratch_shapes=[pltpu.VMEM((D_MODEL, GMLP_WIDTH), BF16)],
        compiler_params=_params("arbitrary"),
        name="proj_v",
    )(x_bf, w_in, ln_g, ln_b)


def _rope(x, cos, sin_signed, first_half):
    outs = []
    for c in range(x.shape[1] // LANES):
        xs = x[:, c * LANES:(c + 1) * LANES]
        swapped = jnp.where(first_half, pltpu.roll(xs, LANES - HEAD_DIM // 2, 1),
                            pltpu.roll(xs, HEAD_DIM // 2, 1))
        outs.append(xs * cos + swapped * sin_signed)
    return outs


def _duplicate_heads(cols, low_half):
    outs = []
    for xs in cols:
        other = pltpu.roll(xs, HEAD_DIM, 1)
        outs.append(jnp.where(low_half, xs, other))
        outs.append(jnp.where(low_half, other, xs))
    return outs


def _head_masks(shape):
    lane = lax.broadcasted_iota(jnp.int32, shape, 1)
    return (lane % HEAD_DIM) < (HEAD_DIM // 2), (lane % LANES) < HEAD_DIM


def _proj_q_kernel(n_tiles, x_ref, w_ref, cos_ref, sin_ref, q_ref, wbf_ref, acc0_ref, acc1_ref):
    step = pl.program_id(0)
    _cast_weight_once(w_ref, wbf_ref, step)

    def epilogue(acc_ref, rows):
        first_half, low_half = _head_masks((EPILOGUE_ROWS, LANES))
        cols = _rope(acc_ref[rows, :], cos_ref[rows, :] * SCORE_SCALE, sin_ref[rows, :] * SCORE_SCALE, first_half)
        padded = []
        for c in cols:
            padded.append(jnp.where(low_half, c, 0.0))
            padded.append(jnp.where(low_half, 0.0, c))
        q_ref[rows, :] = jnp.concatenate(padded, axis=1).astype(BF16)

    _deferred_epilogue(step, n_tiles, (acc0_ref, acc1_ref), lambda: _dot(x_ref[...], wbf_ref[...]), epilogue)


def _proj_kv_kernel(n_tiles, x_ref, w_ref, cos_ref, sin_ref, k_ref, v_ref, wbf_ref, acc0_ref, acc1_ref):
    step = pl.program_id(0)
    _cast_weight_once(w_ref, wbf_ref, step)

    def epilogue(acc_ref, rows):
        first_half, low_half = _head_masks((EPILOGUE_ROWS, LANES))
        acc = acc_ref[rows, :]
        k_cols = _rope(acc[:, :KV_WIDTH], cos_ref[rows, :], sin_ref[rows, :], first_half)
        v_cols = [acc[:, KV_WIDTH + c * LANES:KV_WIDTH + (c + 1) * LANES] for c in range(KV_WIDTH // LANES)]
        k_ref[rows, :] = jnp.concatenate(_duplicate_heads(k_cols, low_half), axis=1).astype(BF16)
        v_ref[rows, :] = jnp.concatenate(_duplicate_heads(v_cols, low_half), axis=1).astype(BF16)

    _deferred_epilogue(step, n_tiles, (acc0_ref, acc1_ref), lambda: _dot(x_ref[...], wbf_ref[...]), epilogue)


def _rope_table_spec(bm, tiles_per_seq):
    return pl.BlockSpec((bm, LANES), lambda s: (jnp.maximum(s - 1, 0) % tiles_per_seq, 0))


def _proj_q(x_bf, w_in, layer, cos_t, sin_t, bm, seq):
    t = x_bf.shape[0]
    n_tiles = t // bm
    return pl.pallas_call(
        functools.partial(_proj_q_kernel, n_tiles),
        grid=(n_tiles + 1,),
        in_specs=[
            pl.BlockSpec((bm, D_MODEL), _cur_tile(n_tiles)),
            _w_in_spec(layer, ATTN_WIDTH, OFF_Q // ATTN_WIDTH),
            _rope_table_spec(bm, seq // bm),
            _rope_table_spec(bm, seq // bm),
        ],
        out_specs=pl.BlockSpec((bm, 2 * ATTN_WIDTH), _done_tile),
        out_shape=jax.ShapeDtypeStruct((t, 2 * ATTN_WIDTH), BF16),
        scratch_shapes=[pltpu.VMEM((D_MODEL, ATTN_WIDTH), BF16)] + _acc_scratch(bm, ATTN_WIDTH),
        compiler_params=_params("arbitrary"),
        name="proj_q",
    )(x_bf, w_in, cos_t, sin_t)


def _proj_kv(x_bf, w_in, layer, cos_t, sin_t, bm, seq):
    t = x_bf.shape[0]
    n_tiles = t // bm
    return pl.pallas_call(
        functools.partial(_proj_kv_kernel, n_tiles),
        grid=(n_tiles + 1,),
        in_specs=[
            pl.BlockSpec((bm, D_MODEL), _cur_tile(n_tiles)),
            _w_in_spec(layer, 2 * KV_WIDTH, OFF_K // (2 * KV_WIDTH)),
            _rope_table_spec(bm, seq // bm),
            _rope_table_spec(bm, seq // bm),
        ],
        out_specs=[
            pl.BlockSpec((bm, 2 * KV_WIDTH), _done_tile),
            pl.BlockSpec((bm, 2 * KV_WIDTH), _done_tile),
        ],
        out_shape=[
            jax.ShapeDtypeStruct((t, 2 * KV_WIDTH), BF16),
            jax.ShapeDtypeStruct((t, 2 * KV_WIDTH), BF16),
        ],
        scratch_shapes=[pltpu.VMEM((D_MODEL, 2 * KV_WIDTH), BF16)] + _acc_scratch(bm, 2 * KV_WIDTH),
        compiler_params=_params("arbitrary"),
        name="proj_kv",
    )(x_bf, w_in, cos_t, sin_t)


def _proj_gate_kernel(n_tiles, m_tiles, n_side, x_ref, w0_ref, w1_ref, b_ref, *refs):
    side_in, o_ref, side_out = refs[:n_side], refs[n_side], refs[n_side + 1:2 * n_side + 1]
    wbf_ref, acc0_ref, acc1_ref = refs[2 * n_side + 1:]
    step = pl.program_id(0)
    half = w0_ref.shape[1]

    @pl.when((step % m_tiles == 0) & (step < n_tiles))
    def _():
        wbf_ref[:, :half] = w0_ref[...].astype(BF16)
        wbf_ref[:, half:] = w1_ref[...].astype(BF16)

    def compute():
        for src, dst in zip(side_in, side_out):
            dst[...] = src[...].astype(BF16)
        return _dot(x_ref[...], wbf_ref[...])

    def epilogue(acc_ref, rows):
        o_ref[rows, :] = jax.nn.sigmoid(acc_ref[rows, :] + b_ref[...])

    _deferred_epilogue(step, n_tiles, (acc0_ref, acc1_ref), compute, epilogue)


def _proj_gate(x_bf, w_in, layer, b_gate, side_weights, bm, bn):
    t = x_bf.shape[0]
    m_tiles = t // bm
    n_tiles = m_tiles * (GATE_WIDTH // bn)
    half = bn // 2
    first = OFF_GATE // half
    cur = lambda s: jnp.minimum(s, n_tiles - 1)
    done = lambda s: jnp.maximum(s - 1, 0)
    slab = lambda w: (w.shape[1] // n_tiles, w.shape[2])
    return pl.pallas_call(
        functools.partial(_proj_gate_kernel, n_tiles, m_tiles, len(side_weights)),
        grid=(n_tiles + 1,),
        in_specs=[
            pl.BlockSpec((bm, D_MODEL), lambda s: (cur(s) % m_tiles, 0)),
            pl.BlockSpec((None, D_MODEL, half), lambda s: (layer, 0, first + 2 * (cur(s) // m_tiles))),
            pl.BlockSpec((None, D_MODEL, half), lambda s: (layer, 0, first + 2 * (cur(s) // m_tiles) + 1)),
            pl.BlockSpec((1, bn), lambda s: (0, done(s) // m_tiles)),
        ] + [pl.BlockSpec((None,) + slab(w), lambda s: (layer, cur(s), 0)) for w in side_weights],
        out_specs=[pl.BlockSpec((bm, bn), lambda s: (done(s) % m_tiles, done(s) // m_tiles))]
        + [pl.BlockSpec(slab(w), lambda s: (cur(s), 0)) for w in side_weights],
        out_shape=[jax.ShapeDtypeStruct((t, GATE_WIDTH), F32)]
        + [jax.ShapeDtypeStruct(w.shape[1:], BF16) for w in side_weights],
        scratch_shapes=[pltpu.VMEM((D_MODEL, bn), BF16)] + _acc_scratch(bm, bn),
        compiler_params=_params("arbitrary"),
        name="proj_gate",
    )(x_bf, w_in, w_in, b_gate, *side_weights)


def _swa_kernel(sink_ref, q_ref, kp_ref, kc_ref, vp_ref, vc_ref, o_ref):
    n = pl.program_id(1)
    pair_rows = 2 * WINDOW
    lane = lax.broadcasted_iota(jnp.int32, (WINDOW, LANES), 1)
    low_half = lane < HEAD_DIM
    row = lax.broadcasted_iota(jnp.int32, (pair_rows, WINDOW), 0)
    key = lax.broadcasted_iota(jnp.int32, (pair_rows, WINDOW), 1)
    in_own_block = key <= (row % WINDOW)
    top_rows = lax.broadcasted_iota(jnp.int32, (pair_rows, 1), 0) < WINDOW
    first_key = jnp.where(n > 0, 0, WINDOW)
    first_bias = jnp.where(lax.broadcasted_iota(jnp.int32, (1, WINDOW), 1) >= first_key, 0.0, -jnp.inf)
    for blk in range(SWA_BLOCKS):
        own = slice(blk * WINDOW, (blk + 1) * WINDOW)
        outs = []
        for p in range(N_Q_HEADS // 2):
            g = (2 * p) // (N_Q_HEADS // N_KV_HEADS)
            kv_cols = slice(g * LANES, (g + 1) * LANES)
            if blk == 0:
                k_prev, v_prev = kp_ref[:, kv_cols], vp_ref[:, kv_cols]
            else:
                before = slice((blk - 1) * WINDOW, blk * WINDOW)
                k_prev, v_prev = kc_ref[before, kv_cols], vc_ref[before, kv_cols]
            qs = jnp.concatenate([q_ref[own, 2 * p * LANES:(2 * p + 1) * LANES],
                                  q_ref[own, (2 * p + 1) * LANES:(2 * p + 2) * LANES]], axis=0)
            s_prev = _dot_nt(qs, k_prev)
            if blk == 0:
                s_prev = s_prev + first_bias
            s = jnp.where(in_own_block, _dot_nt(qs, kc_ref[own, kv_cols]), s_prev)
            sink = jnp.where(top_rows, sink_ref[2 * p], sink_ref[2 * p + 1])
            m = jnp.maximum(jnp.max(s, axis=-1, keepdims=True), sink)
            e = jnp.exp(s - m)
            denom = jnp.sum(e, axis=-1, keepdims=True) + jnp.exp(sink - m)
            e_own = jnp.where(in_own_block, e, 0.0).astype(BF16)
            e_prev = jnp.where(in_own_block, 0.0, e).astype(BF16)
            o2 = (_dot(e_own, vc_ref[own, kv_cols]) + _dot(e_prev, v_prev)) / denom
            outs.append(jnp.where(low_half, o2[:WINDOW], o2[WINDOW:]))
        o_ref[own, :] = jnp.concatenate(outs, axis=1).astype(BF16)


def _swa(q, k2, v2, sinks, bsz, seq):
    steps_per_seq = seq // (SWA_BLOCKS * WINDOW)
    cur = lambda b, n: (b * steps_per_seq + n, 0)
    prev = lambda b, n: (SWA_BLOCKS * (b * steps_per_seq + n) - jnp.minimum(n, 1), 0)
    return pl.pallas_call(
        _swa_kernel,
        grid=(bsz, steps_per_seq),
        in_specs=[
            pl.BlockSpec(memory_space=pltpu.SMEM),
            pl.BlockSpec((SWA_BLOCKS * WINDOW, 2 * ATTN_WIDTH), cur),
            pl.BlockSpec((WINDOW, 2 * KV_WIDTH), prev),
            pl.BlockSpec((SWA_BLOCKS * WINDOW, 2 * KV_WIDTH), cur),
            pl.BlockSpec((WINDOW, 2 * KV_WIDTH), prev),
            pl.BlockSpec((SWA_BLOCKS * WINDOW, 2 * KV_WIDTH), cur),
        ],
        out_specs=pl.BlockSpec((SWA_BLOCKS * WINDOW, ATTN_WIDTH), cur),
        out_shape=jax.ShapeDtypeStruct((bsz * seq, ATTN_WIDTH), BF16),
        compiler_params=_params("parallel", "parallel"),
        name="swa",
    )(sinks, q, k2, k2, v2, v2)


def _branch_kernel(gu_ref, vn_ref, at_ref, ga_ref, gb_ref, ws_ref, bs_ref, wa_ref, wb_ref, o_ref, gated_ref):
    bm = gu_ref.shape[0]
    t_idx = lax.broadcasted_iota(jnp.int32, (CHUNK, CHUNK), 0)
    s_idx = lax.broadcasted_iota(jnp.int32, (CHUNK, CHUNK), 1)
    causal = s_idx <= t_idx
    bs = bs_ref[...]
    for g in range(GMLP_GROUPS):
        w = jnp.where(causal, ws_ref[g], 0.0).astype(BF16)
        bias = bs[:, g:g + 1]
        cols = slice(g * LANES, (g + 1) * LANES)
        for c in range(bm // CHUNK):
            rows = slice(c * CHUNK, (c + 1) * CHUNK)
            mixed = _dot(w, vn_ref[rows, cols]) + bias
            gated_ref[rows, cols] = (gu_ref[rows, cols] * mixed).astype(BF16)
    ya = _dot(gated_ref[...], wa_ref[...])
    yb = _dot(at_ref[...], wb_ref[...])
    o_ref[...] = (ga_ref[...] * ya + gb_ref[...] * yb).astype(BF16)


def _branch(gu, vn, attn, gates, w_s, b_s_t, w_a, w_b, bm):
    t = gu.shape[0]
    row = lambda i: (i, 0)
    const2 = lambda i: (0, 0)
    return pl.pallas_call(
        _branch_kernel,
        grid=(t // bm,),
        in_specs=[
            pl.BlockSpec((bm, GMLP_WIDTH), row),
            pl.BlockSpec((bm, GMLP_WIDTH), row),
            pl.BlockSpec((bm, ATTN_WIDTH), row),
            pl.BlockSpec((bm, D_MODEL), lambda i: (i, 0)),
            pl.BlockSpec((bm, D_MODEL), lambda i: (i, 1)),
            pl.BlockSpec((GMLP_GROUPS, CHUNK, CHUNK), lambda i: (0, 0, 0)),
            pl.BlockSpec((CHUNK, GMLP_GROUPS), const2),
            pl.BlockSpec((GMLP_WIDTH, D_MODEL), const2),
            pl.BlockSpec((ATTN_WIDTH, D_MODEL), const2),
        ],
        out_specs=pl.BlockSpec((bm, D_MODEL), row),
        out_shape=jax.ShapeDtypeStruct((t, D_MODEL), BF16),
        scratch_shapes=[pltpu.VMEM((bm, GMLP_WIDTH), BF16)],
        compiler_params=_params("parallel"),
        name="branch",
    )(gu, vn, attn, gates, gates, w_s, b_s_t, w_a, w_b)


def _post_norm(g_ref, b_ref, o_ref, obf_ref):
    def epilogue(z_ref, rows):
        y = _layer_norm(z_ref[rows, :], g_ref[...], b_ref[...])
        o_ref[rows, :] = y
        obf_ref[rows, :] = y.astype(BF16)
    return epilogue


def _out_proj_kernel(n_tiles, m_ref, x_ref, wo_ref, g_ref, b_ref, o_ref, obf_ref, z0_ref, z1_ref):
    _deferred_epilogue(pl.program_id(0), n_tiles, (z0_ref, z1_ref),
                       lambda: ALPHA * x_ref[...] + _dot(m_ref[...], wo_ref[...]),
                       _post_norm(g_ref, b_ref, o_ref, obf_ref))


def _out_proj(merged, x, w_o, ln_g, ln_b, bm):
    t = x.shape[0]
    n_tiles = t // bm
    const2 = lambda s: (0, 0)
    return pl.pallas_call(
        functools.partial(_out_proj_kernel, n_tiles),
        grid=(n_tiles + 1,),
        in_specs=[
            pl.BlockSpec((bm, D_MODEL), _cur_tile(n_tiles)),
            pl.BlockSpec((bm, D_MODEL), _cur_tile(n_tiles)),
            pl.BlockSpec((D_MODEL, D_MODEL), const2),
            pl.BlockSpec((1, D_MODEL), const2),
            pl.BlockSpec((1, D_MODEL), const2),
        ],
        out_specs=[pl.BlockSpec((bm, D_MODEL), _done_tile), pl.BlockSpec((bm, D_MODEL), _done_tile)],
        out_shape=[jax.ShapeDtypeStruct((t, D_MODEL), F32), jax.ShapeDtypeStruct((t, D_MODEL), BF16)],
        scratch_shapes=_acc_scratch(bm, D_MODEL),
        compiler_params=_params("arbitrary"),
        name="out_proj",
    )(merged, x, w_o, ln_g, ln_b)


def _mem_kv_kernel(m_ref, w_ref, o_ref):
    o_ref[...] = _dot(m_ref[...].astype(BF16), w_ref[...].astype(BF16)).astype(BF16)


def _mem_kv(mem2d, w_xkv, layer):
    rows = mem2d.shape[0]
    return pl.pallas_call(
        _mem_kv_kernel,
        grid=(2,),
        in_specs=[
            pl.BlockSpec((rows, D_MODEL), lambda j: (0, 0)),
            pl.BlockSpec((None, D_MODEL, X_WIDTH), lambda j: (layer, 0, j)),
        ],
        out_specs=pl.BlockSpec((rows, X_WIDTH), lambda j: (0, j)),
        out_shape=jax.ShapeDtypeStruct((rows, 2 * X_WIDTH), BF16),
        compiler_params=_params("parallel"),
        name="mem_kv",
    )(mem2d, w_xkv)


def _xattn_kernel(n_tiles, xbf_ref, x_ref, k_ref, v_ref, wq_ref, wo_ref, g_ref, b_ref, o_ref, obf_ref,
                  z0_ref, z1_ref):
    def compute_z():
        q = _dot(xbf_ref[...], wq_ref[...]).astype(BF16)
        scale = X_HEAD_DIM ** -0.5
        outs = []
        for h in range(X_HEADS):
            cols = slice(h * X_HEAD_DIM, (h + 1) * X_HEAD_DIM)
            s = _dot_nt(q[:, cols], k_ref[:, cols]) * scale
            m = jnp.max(s, axis=-1, keepdims=True)
            p = jnp.exp(s - m)
            denom = jnp.sum(p, axis=-1, keepdims=True)
            outs.append(_dot(p.astype(BF16), v_ref[:, cols]) / denom)
        o = jnp.concatenate(outs, axis=1).astype(BF16)
        return ALPHA * x_ref[...] + _dot(o, wo_ref[...])

    _deferred_epilogue(pl.program_id(0), n_tiles, (z0_ref, z1_ref), compute_z,
                       _post_norm(g_ref, b_ref, o_ref, obf_ref))


def _xattn(x_bf, x, kv, w_xq, w_xo, ln_g, ln_b, bm, seq):
    t = x.shape[0]
    n_tiles = t // bm
    tiles_per_seq = seq // bm
    batch = lambda s: jnp.minimum(s, n_tiles - 1) // tiles_per_seq
    const2 = lambda s: (0, 0)
    return pl.pallas_call(
        functools.partial(_xattn_kernel, n_tiles),
        grid=(n_tiles + 1,),
        in_specs=[
            pl.BlockSpec((bm, D_MODEL), _cur_tile(n_tiles)),
            pl.BlockSpec((bm, D_MODEL), _cur_tile(n_tiles)),
            pl.BlockSpec((MEM_LEN, X_WIDTH), lambda s: (batch(s), 0)),
            pl.BlockSpec((MEM_LEN, X_WIDTH), lambda s: (batch(s), 1)),
            pl.BlockSpec((D_MODEL, X_WIDTH), const2),
            pl.BlockSpec((X_WIDTH, D_MODEL), const2),
            pl.BlockSpec((1, D_MODEL), const2),
            pl.BlockSpec((1, D_MODEL), const2),
        ],
        out_specs=[pl.BlockSpec((bm, D_MODEL), _done_tile), pl.BlockSpec((bm, D_MODEL), _done_tile)],
        out_shape=[jax.ShapeDtypeStruct((t, D_MODEL), F32), jax.ShapeDtypeStruct((t, D_MODEL), BF16)],
        scratch_shapes=_acc_scratch(bm, D_MODEL),
        compiler_params=_params("arbitrary"),
        name="xattn",
    )(x_bf, x, kv, kv, w_xq, w_xo, ln_g, ln_b)


def _mlp_kernel(xbf_ref, x_ref, wu_ref, wd_ref, g_ref, b_ref, o_ref, obf_ref, acc_ref):
    f = pl.program_id(1)

    @pl.when(f == 0)
    def _():
        acc_ref[...] = jnp.zeros_like(acc_ref)

    h = jnp.maximum(_dot(xbf_ref[...], wu_ref[...]), 0.0)
    acc_ref[...] += _dot((h * h).astype(BF16), wd_ref[...])

    @pl.when(f == pl.num_programs(1) - 1)
    def _():
        z = ALPHA * x_ref[...] + acc_ref[...]
        y = _layer_norm(z, g_ref[...], b_ref[...])
        o_ref[...] = y
        obf_ref[...] = y.astype(BF16)


def _mlp(x_bf, x, w_up, w_down, ln_g, ln_b, bm, bf):
    t = x.shape[0]
    n_f = D_FF // bf
    row = lambda i, f: (i, 0)
    const2 = lambda i, f: (0, 0)
    late_row = lambda i, f: (jnp.where(f >= n_f // 2, i, jnp.maximum(i - 1, 0)), 0)
    return pl.pallas_call(
        _mlp_kernel,
        grid=(t // bm, n_f),
        in_specs=[
            pl.BlockSpec((bm, D_MODEL), row),
            pl.BlockSpec((bm, D_MODEL), late_row),
            pl.BlockSpec((D_MODEL, bf), lambda i, f: (0, f)),
            pl.BlockSpec((bf, D_MODEL), lambda i, f: (f, 0)),
            pl.BlockSpec((1, D_MODEL), const2),
            pl.BlockSpec((1, D_MODEL), const2),
        ],
        out_specs=[pl.BlockSpec((bm, D_MODEL), row), pl.BlockSpec((bm, D_MODEL), row)],
        out_shape=[jax.ShapeDtypeStruct((t, D_MODEL), F32), jax.ShapeDtypeStruct((t, D_MODEL), BF16)],
        scratch_shapes=[pltpu.VMEM((bm, D_MODEL), F32)],
        compiler_params=_params("parallel", "arbitrary"),
        name="mlp",
    )(x_bf, x, w_up, w_down, ln_g, ln_b)


def _rope_tables(seq):
    inv = 1.0 / (ROPE_THETA ** (jnp.arange(0, HEAD_DIM, 2, dtype=F32) / HEAD_DIM))
    ang = jnp.arange(seq, dtype=F32)[:, None] * inv[None, :]
    cos, sin = jnp.cos(ang), jnp.sin(ang)
    reps = LANES // HEAD_DIM
    cos_t = jnp.tile(jnp.concatenate([cos, cos], axis=1), (1, reps))
    sin_t = jnp.tile(jnp.concatenate([-sin, sin], axis=1), (1, reps))
    return cos_t, sin_t


def kernel(x, mem, w_in, b_gate, ln_v_g, ln_v_b, w_s, b_s, sinks, w_br_a, w_br_b, w_o, ln1_g, ln1_b,
           w_xq, w_xkv, w_xo, ln2_g, ln2_b, w_up, w_down, ln3_g, ln3_b):
    bsz, seq, d = x.shape
    t = bsz * seq
    cos_t, sin_t = _rope_tables(seq)
    xf = x.reshape(t, d)
    mem2d = mem.reshape(bsz * MEM_LEN, d)
    row = lambda a: a.reshape(1, -1)
    for l in range(DEPTH):
        if l == 0:
            gu, xb = _proj_u_from_f32(xf, w_in, l, bm=512)
        else:
            gu = _proj_u(xb, w_in, l, bm=1024)
        vn = _proj_v(xb, w_in, l, row(ln_v_g[l]), row(ln_v_b[l]), bm=1024)
        q = _proj_q(xb, w_in, l, cos_t, sin_t, bm=1024, seq=seq)
        k2, v2 = _proj_kv(xb, w_in, l, cos_t, sin_t, bm=1024, seq=seq)
        gates, w_a, w_b, w_o_bf, w_up_bf, w_down_bf = _proj_gate(
            xb, w_in, l, row(b_gate[l]), [w_br_a, w_br_b, w_o, w_up, w_down], bm=1024, bn=1024)
        attn = _swa(q, k2, v2, sinks[l], bsz, seq)
        merged = _branch(gu, vn, attn, gates, w_s[l], b_s[l].T, w_a, w_b, bm=512)
        xf, xb = _out_proj(merged, xf, w_o_bf, row(ln1_g[l]), row(ln1_b[l]), bm=512)
        kv = _mem_kv(mem2d, w_xkv, l)
        xf, xb = _xattn(xb, xf, kv, w_xq[l].astype(BF16), w_xo[l].astype(BF16), row(ln2_g[l]),
                        row(ln2_b[l]), bm=512, seq=seq)
        xf, xb = _mlp(xb, xf, w_up_bf, w_down_bf, row(ln3_g[l]), row(ln3_b[l]), bm=512, bf=1024)
    return xf.reshape(bsz, seq, d)
```

```python
import functools

import jax
import jax.numpy as jnp
from jax import lax
from jax.experimental import pallas as pl
from jax.experimental.pallas import tpu as pltpu

D_MODEL = 2048
DEPTH = 2
MEM_LEN = 256
CHUNK = 128
GMLP_WIDTH = 1024
GMLP_GROUPS = 8
N_Q_HEADS = 16
N_KV_HEADS = 4
HEAD_DIM = 64
ATTN_WIDTH = N_Q_HEADS * HEAD_DIM
KV_WIDTH = N_KV_HEADS * HEAD_DIM
WINDOW = 128
ROPE_THETA = 10000.0
X_HEADS = 4
X_HEAD_DIM = 128
X_WIDTH = X_HEADS * X_HEAD_DIM
D_FF = 4 * D_MODEL
LN_EPS = 1e-5
ALPHA = (2 * DEPTH) ** 0.25

OFF_V = GMLP_WIDTH
OFF_Q = 2 * GMLP_WIDTH
OFF_K = OFF_Q + ATTN_WIDTH
OFF_GATE = OFF_K + 2 * KV_WIDTH
GATE_WIDTH = 2 * D_MODEL

LANES = 128
VMEM_LIMIT_BYTES = 56 * 1024 * 1024
SWA_BLOCKS = 4
MERGE_COLS = 512
EPILOGUE_ROWS = 32
SCORE_SCALE = HEAD_DIM ** -0.5

BF16 = jnp.bfloat16
F32 = jnp.float32


def _params(*semantics):
    return pltpu.CompilerParams(dimension_semantics=semantics, vmem_limit_bytes=VMEM_LIMIT_BYTES)


def _dot(a, b):
    return jnp.dot(a, b, preferred_element_type=F32)


def _dot_nt(a, b):
    return lax.dot_general(a, b, (((1,), (1,)), ((), ())), preferred_element_type=F32)


def _layer_norm(z, g, b):
    mu = jnp.mean(z, axis=-1, keepdims=True)
    zc = z - mu
    var = jnp.mean(zc * zc, axis=-1, keepdims=True)
    return zc * lax.rsqrt(var + LN_EPS) * g + b


def _gelu(x):
    return jax.nn.gelu(x, approximate=True)


def _cast_weight_once(w_ref, wbf_ref, step):
    @pl.when(step == 0)
    def _():
        wbf_ref[...] = w_ref[...].astype(BF16)


def _deferred_epilogue(step, n_tiles, acc_refs, compute, epilogue):
    def run_epilogue(acc_ref):
        for r in range(0, acc_ref.shape[0], EPILOGUE_ROWS):
            epilogue(acc_ref, slice(r, r + EPILOGUE_ROWS))

    @pl.when(step == 0)
    def _():
        acc_refs[0][...] = compute()

    for parity in range(2):
        @pl.when((step > 0) & (step < n_tiles) & (step % 2 == parity))
        def _():
            run_epilogue(acc_refs[1 - parity])
            acc_refs[parity][...] = compute()

    @pl.when(step == n_tiles)
    def _():
        run_epilogue(acc_refs[1 - n_tiles % 2])


def _cur_tile(n_tiles):
    return lambda s: (jnp.minimum(s, n_tiles - 1), 0)


def _done_tile(s):
    return (jnp.maximum(s - 1, 0), 0)


def _acc_scratch(bm, width):
    return [pltpu.VMEM((bm, width), F32), pltpu.VMEM((bm, width), F32)]


def _side_cast_specs(side_weights, layer, n_slabs, slab_index):
    in_specs, out_specs, out_shapes = [], [], []
    for w, col0 in side_weights:
        rows = w.shape[1] // n_slabs
        in_specs.append(pl.BlockSpec((None, rows, w.shape[2]), lambda s: (layer, slab_index(s), 0)))
        out_specs.append(pl.BlockSpec((rows, w.shape[2] - col0), lambda s: (slab_index(s), 0)))
        out_shapes.append(jax.ShapeDtypeStruct((w.shape[1], w.shape[2] - col0), BF16))
    return in_specs, out_specs, out_shapes


def _cast_side(side_in, side_out):
    for src, dst in zip(side_in, side_out):
        dst[...] = src[:, src.shape[1] - dst.shape[1]:].astype(BF16)


def _split_refs(refs, n_side, n_out):
    return (refs[:n_side], refs[n_side:n_side + n_out], refs[n_side + n_out:2 * n_side + n_out],
            refs[2 * n_side + n_out:])


def _w_in_spec(layer, width, col_block):
    return pl.BlockSpec((None, D_MODEL, width), lambda s: (layer, 0, col_block))


def _proj_u_kernel(n_side, x_ref, w_ref, *refs):
    side_in, (gu_ref,), side_out, (wbf_ref,) = _split_refs(refs, n_side, 1)
    _cast_weight_once(w_ref, wbf_ref, pl.program_id(0))
    _cast_side(side_in, side_out)
    gu_ref[...] = _gelu(_dot(x_ref[...], wbf_ref[...]))


def _proj_u_from_f32_kernel(n_side, x_ref, w_ref, *refs):
    side_in, (gu_ref, xb_ref), side_out, (wbf_ref,) = _split_refs(refs, n_side, 2)
    _cast_weight_once(w_ref, wbf_ref, pl.program_id(0))
    _cast_side(side_in, side_out)
    xb = x_ref[...].astype(BF16)
    xb_ref[...] = xb
    gu_ref[...] = _gelu(_dot(xb, wbf_ref[...]))


def _proj_v_kernel(n_side, x_ref, w_ref, g_ref, b_ref, *refs):
    side_in, (vn_ref,), side_out, (wbf_ref,) = _split_refs(refs, n_side, 1)
    _cast_weight_once(w_ref, wbf_ref, pl.program_id(0))
    _cast_side(side_in, side_out)
    v = _gelu(_dot(x_ref[...], wbf_ref[...]))
    vn_ref[...] = _layer_norm(v, g_ref[...], b_ref[...]).astype(BF16)


def _proj_u(x, w_in, layer, side_weights, bm):
    t = x.shape[0]
    n_steps = t // bm
    from_f32 = x.dtype == F32
    side_in, side_out, side_shapes = _side_cast_specs(side_weights, layer, n_steps, lambda s: s)
    rows = lambda width: pl.BlockSpec((bm, width), lambda i: (i, 0))
    out_specs, out_shapes = [rows(GMLP_WIDTH)], [jax.ShapeDtypeStruct((t, GMLP_WIDTH), F32)]
    if from_f32:
        out_specs.append(rows(D_MODEL))
        out_shapes.append(jax.ShapeDtypeStruct((t, D_MODEL), BF16))
    kernel_fn = _proj_u_from_f32_kernel if from_f32 else _proj_u_kernel
    return pl.pallas_call(
        functools.partial(kernel_fn, len(side_weights)),
        grid=(n_steps,),
        in_specs=[rows(D_MODEL), _w_in_spec(layer, GMLP_WIDTH, 0)] + side_in,
        out_specs=out_specs + side_out,
        out_shape=out_shapes + side_shapes,
        scratch_shapes=[pltpu.VMEM((D_MODEL, GMLP_WIDTH), BF16)],
        compiler_params=_params("arbitrary"),
        name="proj_u_f32" if from_f32 else "proj_u",
    )(x, w_in, *[w for w, _ in side_weights])


def _proj_v(x_bf, w_in, layer, ln_g, ln_b, side_weights, bm):
    t = x_bf.shape[0]
    n_steps = t // bm
    side_in, side_out, side_shapes = _side_cast_specs(side_weights, layer, n_steps, lambda s: s)
    return pl.pallas_call(
        functools.partial(_proj_v_kernel, len(side_weights)),
        grid=(n_steps,),
        in_specs=[
            pl.BlockSpec((bm, D_MODEL), lambda i: (i, 0)),
            _w_in_spec(layer, GMLP_WIDTH, OFF_V // GMLP_WIDTH),
            pl.BlockSpec((1, GMLP_WIDTH), lambda i: (0, 0)),
            pl.BlockSpec((1, GMLP_WIDTH), lambda i: (0, 0)),
        ] + side_in,
        out_specs=[pl.BlockSpec((bm, GMLP_WIDTH), lambda i: (i, 0))] + side_out,
        out_shape=[jax.ShapeDtypeStruct((t, GMLP_WIDTH), BF16)] + side_shapes,
        scratch_shapes=[pltpu.VMEM((D_MODEL, GMLP_WIDTH), BF16)],
        compiler_params=_params("arbitrary"),
        name="proj_v",
    )(x_bf, w_in, ln_g, ln_b, *[w for w, _ in side_weights])


def _rope(x, cos, sin_signed, first_half):
    outs = []
    for c in range(x.shape[1] // LANES):
        xs = x[:, c * LANES:(c + 1) * LANES]
        swapped = jnp.where(first_half, pltpu.roll(xs, LANES - HEAD_DIM // 2, 1),
                            pltpu.roll(xs, HEAD_DIM // 2, 1))
        outs.append(xs * cos + swapped * sin_signed)
    return outs


def _duplicate_heads(cols, low_half):
    outs = []
    for xs in cols:
        other = pltpu.roll(xs, HEAD_DIM, 1)
        outs.append(jnp.where(low_half, xs, other))
        outs.append(jnp.where(low_half, other, xs))
    return outs


def _head_masks(shape):
    lane = lax.broadcasted_iota(jnp.int32, shape, 1)
    return (lane % HEAD_DIM) < (HEAD_DIM // 2), (lane % LANES) < HEAD_DIM


def _proj_q_kernel(n_tiles, x_ref, w_ref, cos_ref, sin_ref, q_ref, wbf_ref, acc0_ref, acc1_ref):
    step = pl.program_id(0)
    _cast_weight_once(w_ref, wbf_ref, step)

    def epilogue(acc_ref, rows):
        first_half, low_half = _head_masks((EPILOGUE_ROWS, LANES))
        cols = _rope(acc_ref[rows, :], cos_ref[rows, :] * SCORE_SCALE, sin_ref[rows, :] * SCORE_SCALE, first_half)
        padded = []
        for c in cols:
            padded.append(jnp.where(low_half, c, 0.0))
            padded.append(jnp.where(low_half, 0.0, c))
        q_ref[rows, :] = jnp.concatenate(padded, axis=1).astype(BF16)

    _deferred_epilogue(step, n_tiles, (acc0_ref, acc1_ref), lambda: _dot(x_ref[...], wbf_ref[...]), epilogue)


def _proj_kv_kernel(n_tiles, n_side, x_ref, w_ref, cos_ref, sin_ref, *refs):
    side_in, (k_ref, v_ref), side_out, (wbf_ref, acc0_ref, acc1_ref) = _split_refs(refs, n_side, 2)
    step = pl.program_id(0)
    _cast_weight_once(w_ref, wbf_ref, step)

    def compute():
        _cast_side(side_in, side_out)
        return _dot(x_ref[...], wbf_ref[...])

    def epilogue(acc_ref, rows):
        first_half, low_half = _head_masks((EPILOGUE_ROWS, LANES))
        acc = acc_ref[rows, :]
        k_cols = _rope(acc[:, :KV_WIDTH], cos_ref[rows, :], sin_ref[rows, :], first_half)
        v_cols = [acc[:, KV_WIDTH + c * LANES:KV_WIDTH + (c + 1) * LANES] for c in range(KV_WIDTH // LANES)]
        k_ref[rows, :] = jnp.concatenate(_duplicate_heads(k_cols, low_half), axis=1).astype(BF16)
        v_ref[rows, :] = jnp.concatenate(_duplicate_heads(v_cols, low_half), axis=1).astype(BF16)

    _deferred_epilogue(step, n_tiles, (acc0_ref, acc1_ref), compute, epilogue)


def _rope_table_spec(bm, tiles_per_seq):
    return pl.BlockSpec((bm, LANES), lambda s: (jnp.maximum(s - 1, 0) % tiles_per_seq, 0))


def _proj_q(x_bf, w_in, layer, cos_t, sin_t, bm, seq):
    t = x_bf.shape[0]
    n_tiles = t // bm
    return pl.pallas_call(
        functools.partial(_proj_q_kernel, n_tiles),
        grid=(n_tiles + 1,),
        in_specs=[
            pl.BlockSpec((bm, D_MODEL), _cur_tile(n_tiles)),
            _w_in_spec(layer, ATTN_WIDTH, OFF_Q // ATTN_WIDTH),
            _rope_table_spec(bm, seq // bm),
            _rope_table_spec(bm, seq // bm),
        ],
        out_specs=pl.BlockSpec((bm, 2 * ATTN_WIDTH), _done_tile),
        out_shape=jax.ShapeDtypeStruct((t, 2 * ATTN_WIDTH), BF16),
        scratch_shapes=[pltpu.VMEM((D_MODEL, ATTN_WIDTH), BF16)] + _acc_scratch(bm, ATTN_WIDTH),
        compiler_params=_params("arbitrary"),
        name="proj_q",
    )(x_bf, w_in, cos_t, sin_t)


def _proj_kv(x_bf, w_in, layer, cos_t, sin_t, side_weights, bm, seq):
    t = x_bf.shape[0]
    n_tiles = t // bm
    side_in, side_out, side_shapes = _side_cast_specs(side_weights, layer, n_tiles,
                                                      lambda s: jnp.minimum(s, n_tiles - 1))
    return pl.pallas_call(
        functools.partial(_proj_kv_kernel, n_tiles, len(side_weights)),
        grid=(n_tiles + 1,),
        in_specs=[
            pl.BlockSpec((bm, D_MODEL), _cur_tile(n_tiles)),
            _w_in_spec(layer, 2 * KV_WIDTH, OFF_K // (2 * KV_WIDTH)),
            _rope_table_spec(bm, seq // bm),
            _rope_table_spec(bm, seq // bm),
        ] + side_in,
        out_specs=[
            pl.BlockSpec((bm, 2 * KV_WIDTH), _done_tile),
            pl.BlockSpec((bm, 2 * KV_WIDTH), _done_tile),
        ] + side_out,
        out_shape=[
            jax.ShapeDtypeStruct((t, 2 * KV_WIDTH), BF16),
            jax.ShapeDtypeStruct((t, 2 * KV_WIDTH), BF16),
        ] + side_shapes,
        scratch_shapes=[pltpu.VMEM((D_MODEL, 2 * KV_WIDTH), BF16)] + _acc_scratch(bm, 2 * KV_WIDTH),
        compiler_params=_params("arbitrary"),
        name="proj_kv",
    )(x_bf, w_in, cos_t, sin_t, *[w for w, _ in side_weights])


def _swa_kernel(sink_ref, q_ref, kp_ref, kc_ref, vp_ref, vc_ref, o_ref):
    n = pl.program_id(1)
    pair_rows = 2 * WINDOW
    lane = lax.broadcasted_iota(jnp.int32, (WINDOW, LANES), 1)
    low_half = lane < HEAD_DIM
    row = lax.broadcasted_iota(jnp.int32, (pair_rows, WINDOW), 0)
    key = lax.broadcasted_iota(jnp.int32, (pair_rows, WINDOW), 1)
    in_own_block = key <= (row % WINDOW)
    top_rows = lax.broadcasted_iota(jnp.int32, (pair_rows, 1), 0) < WINDOW
    first_key = jnp.where(n > 0, 0, WINDOW)
    first_bias = jnp.where(lax.broadcasted_iota(jnp.int32, (1, WINDOW), 1) >= first_key, 0.0, -jnp.inf)
    for blk in range(SWA_BLOCKS):
        own = slice(blk * WINDOW, (blk + 1) * WINDOW)
        outs = []
        for p in range(N_Q_HEADS // 2):
            g = (2 * p) // (N_Q_HEADS // N_KV_HEADS)
            kv_cols = slice(g * LANES, (g + 1) * LANES)
            if blk == 0:
                k_prev, v_prev = kp_ref[:, kv_cols], vp_ref[:, kv_cols]
            else:
                before = slice((blk - 1) * WINDOW, blk * WINDOW)
                k_prev, v_prev = kc_ref[before, kv_cols], vc_ref[before, kv_cols]
            qs = jnp.concatenate([q_ref[own, 2 * p * LANES:(2 * p + 1) * LANES],
                                  q_ref[own, (2 * p + 1) * LANES:(2 * p + 2) * LANES]], axis=0)
            s_prev = _dot_nt(qs, k_prev)
            if blk == 0:
                s_prev = s_prev + first_bias
            s = jnp.where(in_own_block, _dot_nt(qs, kc_ref[own, kv_cols]), s_prev)
            sink = jnp.where(top_rows, sink_ref[2 * p], sink_ref[2 * p + 1])
            m = jnp.maximum(jnp.max(s, axis=-1, keepdims=True), sink)
            e = jnp.exp(s - m)
            denom = jnp.sum(e, axis=-1, keepdims=True) + jnp.exp(sink - m)
            e_own = jnp.where(in_own_block, e, 0.0).astype(BF16)
            e_prev = jnp.where(in_own_block, 0.0, e).astype(BF16)
            o2 = (_dot(e_own, vc_ref[own, kv_cols]) + _dot(e_prev, v_prev)) / denom
            outs.append(jnp.where(low_half, o2[:WINDOW], o2[WINDOW:]))
        o_ref[own, :] = jnp.concatenate(outs, axis=1).astype(BF16)


def _swa(q, k2, v2, sinks, bsz, seq):
    steps_per_seq = seq // (SWA_BLOCKS * WINDOW)
    cur = lambda b, n: (b * steps_per_seq + n, 0)
    prev = lambda b, n: (SWA_BLOCKS * (b * steps_per_seq + n) - jnp.minimum(n, 1), 0)
    return pl.pallas_call(
        _swa_kernel,
        grid=(bsz, steps_per_seq),
        in_specs=[
            pl.BlockSpec(memory_space=pltpu.SMEM),
            pl.BlockSpec((SWA_BLOCKS * WINDOW, 2 * ATTN_WIDTH), cur),
            pl.BlockSpec((WINDOW, 2 * KV_WIDTH), prev),
            pl.BlockSpec((SWA_BLOCKS * WINDOW, 2 * KV_WIDTH), cur),
            pl.BlockSpec((WINDOW, 2 * KV_WIDTH), prev),
            pl.BlockSpec((SWA_BLOCKS * WINDOW, 2 * KV_WIDTH), cur),
        ],
        out_specs=pl.BlockSpec((SWA_BLOCKS * WINDOW, ATTN_WIDTH), cur),
        out_shape=jax.ShapeDtypeStruct((bsz * seq, ATTN_WIDTH), BF16),
        compiler_params=_params("parallel", "parallel"),
        name="swa",
    )(sinks, q, k2, k2, v2, v2)


def _gate_branch_kernel(xb_ref, gu_ref, vn_ref, at_ref, ws_ref, bs_ref, bg_ref, wa_ref, wb_ref, wg_ref,
                        o_ref, gated_ref):
    bm = gu_ref.shape[0]
    t_idx = lax.broadcasted_iota(jnp.int32, (CHUNK, CHUNK), 0)
    s_idx = lax.broadcasted_iota(jnp.int32, (CHUNK, CHUNK), 1)
    causal = s_idx <= t_idx
    bs = bs_ref[...]
    for g in range(GMLP_GROUPS):
        w = jnp.where(causal, ws_ref[g], 0.0).astype(BF16)
        bias = bs[:, g:g + 1]
        cols = slice(g * LANES, (g + 1) * LANES)
        for c in range(bm // CHUNK):
            rows = slice(c * CHUNK, (c + 1) * CHUNK)
            mixed = _dot(w, vn_ref[rows, cols]) + bias
            gated_ref[rows, cols] = (gu_ref[rows, cols] * mixed).astype(BF16)
    for c in range(D_MODEL // MERGE_COLS):
        cols_a = slice(c * MERGE_COLS, (c + 1) * MERGE_COLS)
        cols_b = slice(D_MODEL + c * MERGE_COLS, D_MODEL + (c + 1) * MERGE_COLS)
        ya = _dot(gated_ref[...], wa_ref[:, cols_a])
        yb = _dot(at_ref[...], wb_ref[:, cols_a])
        ga = jax.nn.sigmoid(_dot(xb_ref[...], wg_ref[:, cols_a]) + bg_ref[:, cols_a])
        gb = jax.nn.sigmoid(_dot(xb_ref[...], wg_ref[:, cols_b]) + bg_ref[:, cols_b])
        o_ref[:, cols_a] = (ga * ya + gb * yb).astype(BF16)


def _gate_branch(x_bf, gu, vn, attn, w_s, b_s_t, b_gate, w_a, w_b, w_g, bm):
    t = gu.shape[0]
    row = lambda i: (i, 0)
    const2 = lambda i: (0, 0)
    return pl.pallas_call(
        _gate_branch_kernel,
        grid=(t // bm,),
        in_specs=[
            pl.BlockSpec((bm, D_MODEL), row),
            pl.BlockSpec((bm, GMLP_WIDTH), row),
            pl.BlockSpec((bm, GMLP_WIDTH), row),
            pl.BlockSpec((bm, ATTN_WIDTH), row),
            pl.BlockSpec((GMLP_GROUPS, CHUNK, CHUNK), lambda i: (0, 0, 0)),
            pl.BlockSpec((CHUNK, GMLP_GROUPS), const2),
            pl.BlockSpec((1, GATE_WIDTH), const2),
            pl.BlockSpec((GMLP_WIDTH, D_MODEL), const2),
            pl.BlockSpec((ATTN_WIDTH, D_MODEL), const2),
            pl.BlockSpec((D_MODEL, GATE_WIDTH), const2),
        ],
        out_specs=pl.BlockSpec((bm, D_MODEL), row),
        out_shape=jax.ShapeDtypeStruct((t, D_MODEL), BF16),
        scratch_shapes=[pltpu.VMEM((bm, GMLP_WIDTH), BF16)],
        compiler_params=_params("parallel"),
        name="gate_branch",
    )(x_bf, gu, vn, attn, w_s, b_s_t, b_gate, w_a, w_b, w_g)


def _post_norm(g_ref, b_ref, o_ref, obf_ref):
    def epilogue(z_ref, rows):
        y = _layer_norm(z_ref[rows, :], g_ref[...], b_ref[...])
        o_ref[rows, :] = y
        obf_ref[rows, :] = y.astype(BF16)
    return epilogue


def _out_proj_kernel(n_tiles, m_ref, x_ref, wo_ref, g_ref, b_ref, o_ref, obf_ref, z0_ref, z1_ref):
    _deferred_epilogue(pl.program_id(0), n_tiles, (z0_ref, z1_ref),
                       lambda: ALPHA * x_ref[...] + _dot(m_ref[...], wo_ref[...]),
                       _post_norm(g_ref, b_ref, o_ref, obf_ref))


def _out_proj(merged, x, w_o, ln_g, ln_b, bm):
    t = x.shape[0]
    n_tiles = t // bm
    const2 = lambda s: (0, 0)
    return pl.pallas_call(
        functools.partial(_out_proj_kernel, n_tiles),
        grid=(n_tiles + 1,),
        in_specs=[
            pl.BlockSpec((bm, D_MODEL), _cur_tile(n_tiles)),
            pl.BlockSpec((bm, D_MODEL), _cur_tile(n_tiles)),
            pl.BlockSpec((D_MODEL, D_MODEL), const2),
            pl.BlockSpec((1, D_MODEL), const2),
            pl.BlockSpec((1, D_MODEL), const2),
        ],
        out_specs=[pl.BlockSpec((bm, D_MODEL), _done_tile), pl.BlockSpec((bm, D_MODEL), _done_tile)],
        out_shape=[jax.ShapeDtypeStruct((t, D_MODEL), F32), jax.ShapeDtypeStruct((t, D_MODEL), BF16)],
        scratch_shapes=_acc_scratch(bm, D_MODEL),
        compiler_params=_params("arbitrary"),
        name="out_proj",
    )(merged, x, w_o, ln_g, ln_b)


def _mem_kv_kernel(m_ref, w_ref, o_ref):
    o_ref[...] = _dot(m_ref[...].astype(BF16), w_ref[...].astype(BF16)).astype(BF16)


def _mem_kv(mem2d, w_xkv, layer):
    rows = mem2d.shape[0]
    return pl.pallas_call(
        _mem_kv_kernel,
        grid=(2,),
        in_specs=[
            pl.BlockSpec((rows, D_MODEL), lambda j: (0, 0)),
            pl.BlockSpec((None, D_MODEL, X_WIDTH), lambda j: (layer, 0, j)),
        ],
        out_specs=pl.BlockSpec((rows, X_WIDTH), lambda j: (0, j)),
        out_shape=jax.ShapeDtypeStruct((rows, 2 * X_WIDTH), BF16),
        compiler_params=_params("parallel"),
        name="mem_kv",
    )(mem2d, w_xkv)


def _xattn_kernel(n_tiles, xbf_ref, x_ref, k_ref, v_ref, wq_ref, wo_ref, g_ref, b_ref, o_ref, obf_ref,
                  z0_ref, z1_ref):
    def compute_z():
        q = _dot(xbf_ref[...], wq_ref[...]).astype(BF16)
        scale = X_HEAD_DIM ** -0.5
        outs = []
        for h in range(X_HEADS):
            cols = slice(h * X_HEAD_DIM, (h + 1) * X_HEAD_DIM)
            s = _dot_nt(q[:, cols], k_ref[:, cols]) * scale
            m = jnp.max(s, axis=-1, keepdims=True)
            p = jnp.exp(s - m)
            denom = jnp.sum(p, axis=-1, keepdims=True)
            outs.append(_dot(p.astype(BF16), v_ref[:, cols]) / denom)
        o = jnp.concatenate(outs, axis=1).astype(BF16)
        return ALPHA * x_ref[...] + _dot(o, wo_ref[...])

    _deferred_epilogue(pl.program_id(0), n_tiles, (z0_ref, z1_ref), compute_z,
                       _post_norm(g_ref, b_ref, o_ref, obf_ref))


def _xattn(x_bf, x, kv, w_xq, w_xo, ln_g, ln_b, bm, seq):
    t = x.shape[0]
    n_tiles = t // bm
    tiles_per_seq = seq // bm
    batch = lambda s: jnp.minimum(s, n_tiles - 1) // tiles_per_seq
    const2 = lambda s: (0, 0)
    return pl.pallas_call(
        functools.partial(_xattn_kernel, n_tiles),
        grid=(n_tiles + 1,),
        in_specs=[
            pl.BlockSpec((bm, D_MODEL), _cur_tile(n_tiles)),
            pl.BlockSpec((bm, D_MODEL), _cur_tile(n_tiles)),
            pl.BlockSpec((MEM_LEN, X_WIDTH), lambda s: (batch(s), 0)),
            pl.BlockSpec((MEM_LEN, X_WIDTH), lambda s: (batch(s), 1)),
            pl.BlockSpec((D_MODEL, X_WIDTH), const2),
            pl.BlockSpec((X_WIDTH, D_MODEL), const2),
            pl.BlockSpec((1, D_MODEL), const2),
            pl.BlockSpec((1, D_MODEL), const2),
        ],
        out_specs=[pl.BlockSpec((bm, D_MODEL), _done_tile), pl.BlockSpec((bm, D_MODEL), _done_tile)],
        out_shape=[jax.ShapeDtypeStruct((t, D_MODEL), F32), jax.ShapeDtypeStruct((t, D_MODEL), BF16)],
        scratch_shapes=_acc_scratch(bm, D_MODEL),
        compiler_params=_params("arbitrary"),
        name="xattn",
    )(x_bf, x, kv, kv, w_xq, w_xo, ln_g, ln_b)


def _mlp_kernel(xbf_ref, x_ref, wu_ref, wd_ref, g_ref, b_ref, o_ref, obf_ref, acc_ref):
    f = pl.program_id(1)

    @pl.when(f == 0)
    def _():
        acc_ref[...] = jnp.zeros_like(acc_ref)

    h = jnp.maximum(_dot(xbf_ref[...], wu_ref[...]), 0.0)
    acc_ref[...] += _dot((h * h).astype(BF16), wd_ref[...])

    @pl.when(f == pl.num_programs(1) - 1)
    def _():
        z = ALPHA * x_ref[...] + acc_ref[...]
        y = _layer_norm(z, g_ref[...], b_ref[...])
        o_ref[...] = y
        obf_ref[...] = y.astype(BF16)


def _mlp(x_bf, x, w_up, w_down, ln_g, ln_b, bm, bf):
    t = x.shape[0]
    n_f = D_FF // bf
    row = lambda i, f: (i, 0)
    const2 = lambda i, f: (0, 0)
    late_row = lambda i, f: (jnp.where(f >= n_f // 2, i, jnp.maximum(i - 1, 0)), 0)
    return pl.pallas_call(
        _mlp_kernel,
        grid=(t // bm, n_f),
        in_specs=[
            pl.BlockSpec((bm, D_MODEL), row),
            pl.BlockSpec((bm, D_MODEL), late_row),
            pl.BlockSpec((D_MODEL, bf), lambda i, f: (0, f)),
            pl.BlockSpec((bf, D_MODEL), lambda i, f: (f, 0)),
            pl.BlockSpec((1, D_MODEL), const2),
            pl.BlockSpec((1, D_MODEL), const2),
        ],
        out_specs=[pl.BlockSpec((bm, D_MODEL), row), pl.BlockSpec((bm, D_MODEL), row)],
        out_shape=[jax.ShapeDtypeStruct((t, D_MODEL), F32), jax.ShapeDtypeStruct((t, D_MODEL), BF16)],
        scratch_shapes=[pltpu.VMEM((bm, D_MODEL), F32)],
        compiler_params=_params("parallel", "arbitrary"),
        name="mlp",
    )(x_bf, x, w_up, w_down, ln_g, ln_b)


def _rope_tables(seq):
    inv = 1.0 / (ROPE_THETA ** (jnp.arange(0, HEAD_DIM, 2, dtype=F32) / HEAD_DIM))
    ang = jnp.arange(seq, dtype=F32)[:, None] * inv[None, :]
    cos, sin = jnp.cos(ang), jnp.sin(ang)
    reps = LANES // HEAD_DIM
    cos_t = jnp.tile(jnp.concatenate([cos, cos], axis=1), (1, reps))
    sin_t = jnp.tile(jnp.concatenate([-sin, sin], axis=1), (1, reps))
    return cos_t, sin_t


def kernel(x, mem, w_in, b_gate, ln_v_g, ln_v_b, w_s, b_s, sinks, w_br_a, w_br_b, w_o, ln1_g, ln1_b,
           w_xq, w_xkv, w_xo, ln2_g, ln2_b, w_up, w_down, ln3_g, ln3_b):
    bsz, seq, d = x.shape
    t = bsz * seq
    cos_t, sin_t = _rope_tables(seq)
    xf = x.reshape(t, d)
    mem2d = mem.reshape(bsz * MEM_LEN, d)
    row = lambda a: a.reshape(1, -1)
    xb = xf
    for l in range(DEPTH):
        out = _proj_u(xb, w_in, l, [(w_down, 0)], bm=512 if l == 0 else 1024)
        if l == 0:
            gu, xb, w_down_bf = out
        else:
            gu, w_down_bf = out
        vn, w_up_bf = _proj_v(xb, w_in, l, row(ln_v_g[l]), row(ln_v_b[l]), [(w_up, 0)], bm=1024)
        q = _proj_q(xb, w_in, l, cos_t, sin_t, bm=1024, seq=seq)
        k2, v2, w_g, w_a, w_b, w_o_bf = _proj_kv(
            xb, w_in, l, cos_t, sin_t, [(w_in, OFF_GATE), (w_br_a, 0), (w_br_b, 0), (w_o, 0)], bm=1024, seq=seq)
        attn = _swa(q, k2, v2, sinks[l], bsz, seq)
        merged = _gate_branch(xb, gu, vn, attn, w_s[l], b_s[l].T, row(b_gate[l]), w_a, w_b, w_g, bm=512)
        xf, xb = _out_proj(merged, xf, w_o_bf, row(ln1_g[l]), row(ln1_b[l]), bm=512)
        kv = _mem_kv(mem2d, w_xkv, l)
        xf, xb = _xattn(xb, xf, kv, w_xq[l].astype(BF16), w_xo[l].astype(BF16), row(ln2_g[l]),
                        row(ln2_b[l]), bm=512, seq=seq)
        xf, xb = _mlp(xb, xf, w_up_bf, w_down_bf, row(ln3_g[l]), row(ln3_b[l]), bm=512, bf=1024)
    return xf.reshape(bsz, seq, d)
```

```python
import functools

import jax
import jax.numpy as jnp
from jax import lax
from jax.experimental import pallas as pl
from jax.experimental.pallas import tpu as pltpu

D_MODEL = 2048
DEPTH = 2
MEM_LEN = 256
CHUNK = 128
GMLP_WIDTH = 1024
GMLP_GROUPS = 8
N_Q_HEADS = 16
N_KV_HEADS = 4
HEAD_DIM = 64
ATTN_WIDTH = N_Q_HEADS * HEAD_DIM
KV_WIDTH = N_KV_HEADS * HEAD_DIM
WINDOW = 128
ROPE_THETA = 10000.0
X_HEADS = 4
X_HEAD_DIM = 128
X_WIDTH = X_HEADS * X_HEAD_DIM
D_FF = 4 * D_MODEL
LN_EPS = 1e-5
ALPHA = (2 * DEPTH) ** 0.25

OFF_V = GMLP_WIDTH
OFF_Q = 2 * GMLP_WIDTH
OFF_K = OFF_Q + ATTN_WIDTH
OFF_GATE = OFF_K + 2 * KV_WIDTH
GATE_WIDTH = 2 * D_MODEL

LANES = 128
VMEM_LIMIT_BYTES = 56 * 1024 * 1024
SWA_BLOCKS = 4
MERGE_COLS = 512
EPILOGUE_ROWS = 32
SCORE_SCALE = HEAD_DIM ** -0.5

BF16 = jnp.bfloat16
F32 = jnp.float32


def _params(*semantics):
    return pltpu.CompilerParams(dimension_semantics=semantics, vmem_limit_bytes=VMEM_LIMIT_BYTES)


def _dot(a, b):
    return jnp.dot(a, b, preferred_element_type=F32)


def _dot_nt(a, b):
    return lax.dot_general(a, b, (((1,), (1,)), ((), ())), preferred_element_type=F32)


def _layer_norm(z, g, b):
    mu = jnp.mean(z, axis=-1, keepdims=True)
    zc = z - mu
    var = jnp.mean(zc * zc, axis=-1, keepdims=True)
    return zc * lax.rsqrt(var + LN_EPS) * g + b


def _gelu(x):
    return jax.nn.gelu(x, approximate=True)


def _cast_weight_once(w_ref, wbf_ref, step):
    @pl.when(step == 0)
    def _():
        wbf_ref[...] = w_ref[...].astype(BF16)


def _deferred_epilogue(step, n_tiles, acc_refs, compute, epilogue):
    def run_epilogue(acc_ref):
        for r in range(0, acc_ref.shape[0], EPILOGUE_ROWS):
            epilogue(acc_ref, slice(r, r + EPILOGUE_ROWS))

    @pl.when(step == 0)
    def _():
        acc_refs[0][...] = compute()

    for parity in range(2):
        @pl.when((step > 0) & (step < n_tiles) & (step % 2 == parity))
        def _():
            run_epilogue(acc_refs[1 - parity])
            acc_refs[parity][...] = compute()

    @pl.when(step == n_tiles)
    def _():
        run_epilogue(acc_refs[1 - n_tiles % 2])


def _cur_tile(n_tiles):
    return lambda s: (jnp.minimum(s, n_tiles - 1), 0)


def _done_tile(s):
    return (jnp.maximum(s - 1, 0), 0)


def _acc_scratch(bm, width):
    return [pltpu.VMEM((bm, width), F32), pltpu.VMEM((bm, width), F32)]


def _side_cast_specs(side_weights, layer, n_slabs, slab_index):
    in_specs, out_specs, out_shapes = [], [], []
    for w, col0 in side_weights:
        rows = w.shape[1] // n_slabs
        in_specs.append(pl.BlockSpec((None, rows, w.shape[2]), lambda s: (layer, slab_index(s), 0)))
        out_specs.append(pl.BlockSpec((rows, w.shape[2] - col0), lambda s: (slab_index(s), 0)))
        out_shapes.append(jax.ShapeDtypeStruct((w.shape[1], w.shape[2] - col0), BF16))
    return in_specs, out_specs, out_shapes


def _cast_side(side_in, side_out):
    for src, dst in zip(side_in, side_out):
        dst[...] = src[:, src.shape[1] - dst.shape[1]:].astype(BF16)


def _split_refs(refs, n_side, n_out):
    return (refs[:n_side], refs[n_side:n_side + n_out], refs[n_side + n_out:2 * n_side + n_out],
            refs[2 * n_side + n_out:])


def _w_in_spec(layer, width, col_block):
    return pl.BlockSpec((None, D_MODEL, width), lambda s: (layer, 0, col_block))


def _proj_u_kernel(n_side, x_ref, w_ref, *refs):
    side_in, (gu_ref,), side_out, (wbf_ref,) = _split_refs(refs, n_side, 1)
    _cast_weight_once(w_ref, wbf_ref, pl.program_id(0))
    _cast_side(side_in, side_out)
    gu_ref[...] = _gelu(_dot(x_ref[...], wbf_ref[...]))


def _proj_u_from_f32_kernel(n_side, x_ref, w_ref, *refs):
    side_in, (gu_ref, xb_ref), side_out, (wbf_ref,) = _split_refs(refs, n_side, 2)
    _cast_weight_once(w_ref, wbf_ref, pl.program_id(0))
    _cast_side(side_in, side_out)
    xb = x_ref[...].astype(BF16)
    xb_ref[...] = xb
    gu_ref[...] = _gelu(_dot(xb, wbf_ref[...]))


def _proj_v_kernel(n_side, x_ref, w_ref, g_ref, b_ref, *refs):
    side_in, (vn_ref,), side_out, (wbf_ref,) = _split_refs(refs, n_side, 1)
    _cast_weight_once(w_ref, wbf_ref, pl.program_id(0))
    _cast_side(side_in, side_out)
    v = _gelu(_dot(x_ref[...], wbf_ref[...]))
    vn_ref[...] = _layer_norm(v, g_ref[...], b_ref[...]).astype(BF16)


def _proj_u(x, w_in, layer, side_weights, bm):
    t = x.shape[0]
    n_steps = t // bm
    from_f32 = x.dtype == F32
    side_in, side_out, side_shapes = _side_cast_specs(side_weights, layer, n_steps, lambda s: s)
    rows = lambda width: pl.BlockSpec((bm, width), lambda i: (i, 0))
    out_specs, out_shapes = [rows(GMLP_WIDTH)], [jax.ShapeDtypeStruct((t, GMLP_WIDTH), F32)]
    if from_f32:
        out_specs.append(rows(D_MODEL))
        out_shapes.append(jax.ShapeDtypeStruct((t, D_MODEL), BF16))
    kernel_fn = _proj_u_from_f32_kernel if from_f32 else _proj_u_kernel
    return pl.pallas_call(
        functools.partial(kernel_fn, len(side_weights)),
        grid=(n_steps,),
        in_specs=[rows(D_MODEL), _w_in_spec(layer, GMLP_WIDTH, 0)] + side_in,
        out_specs=out_specs + side_out,
        out_shape=out_shapes + side_shapes,
        scratch_shapes=[pltpu.VMEM((D_MODEL, GMLP_WIDTH), BF16)],
        compiler_params=_params("arbitrary"),
        name="proj_u_f32" if from_f32 else "proj_u",
    )(x, w_in, *[w for w, _ in side_weights])


def _proj_v(x_bf, w_in, layer, ln_g, ln_b, side_weights, bm):
    t = x_bf.shape[0]
    n_steps = t // bm
    side_in, side_out, side_shapes = _side_cast_specs(side_weights, layer, n_steps, lambda s: s)
    return pl.pallas_call(
        functools.partial(_proj_v_kernel, len(side_weights)),
        grid=(n_steps,),
        in_specs=[
            pl.BlockSpec((bm, D_MODEL), lambda i: (i, 0)),
            _w_in_spec(layer, GMLP_WIDTH, OFF_V // GMLP_WIDTH),
            pl.BlockSpec((1, GMLP_WIDTH), lambda i: (0, 0)),
            pl.BlockSpec((1, GMLP_WIDTH), lambda i: (0, 0)),
        ] + side_in,
        out_specs=[pl.BlockSpec((bm, GMLP_WIDTH), lambda i: (i, 0))] + side_out,
        out_shape=[jax.ShapeDtypeStruct((t, GMLP_WIDTH), BF16)] + side_shapes,
        scratch_shapes=[pltpu.VMEM((D_MODEL, GMLP_WIDTH), BF16)],
        compiler_params=_params("arbitrary"),
        name="proj_v",
    )(x_bf, w_in, ln_g, ln_b, *[w for w, _ in side_weights])


def _rope(x, cos, sin_signed, first_half):
    outs = []
    for c in range(x.shape[1] // LANES):
        xs = x[:, c * LANES:(c + 1) * LANES]
        swapped = jnp.where(first_half, pltpu.roll(xs, LANES - HEAD_DIM // 2, 1),
                            pltpu.roll(xs, HEAD_DIM // 2, 1))
        outs.append(xs * cos + swapped * sin_signed)
    return outs


def _duplicate_heads(cols, low_half):
    outs = []
    for xs in cols:
        other = pltpu.roll(xs, HEAD_DIM, 1)
        outs.append(jnp.where(low_half, xs, other))
        outs.append(jnp.where(low_half, other, xs))
    return outs


def _head_masks(shape):
    lane = lax.broadcasted_iota(jnp.int32, shape, 1)
    return (lane % HEAD_DIM) < (HEAD_DIM // 2), (lane % LANES) < HEAD_DIM


def _proj_q_kernel(n_tiles, x_ref, w_ref, cos_ref, sin_ref, q_ref, wbf_ref, acc0_ref, acc1_ref):
    step = pl.program_id(0)
    _cast_weight_once(w_ref, wbf_ref, step)

    def epilogue(acc_ref, rows):
        first_half, low_half = _head_masks((EPILOGUE_ROWS, LANES))
        cols = _rope(acc_ref[rows, :], cos_ref[rows, :] * SCORE_SCALE, sin_ref[rows, :] * SCORE_SCALE, first_half)
        padded = []
        for c in cols:
            padded.append(jnp.where(low_half, c, 0.0))
            padded.append(jnp.where(low_half, 0.0, c))
        q_ref[rows, :] = jnp.concatenate(padded, axis=1).astype(BF16)

    _deferred_epilogue(step, n_tiles, (acc0_ref, acc1_ref), lambda: _dot(x_ref[...], wbf_ref[...]), epilogue)


def _proj_kv_kernel(n_tiles, n_side, x_ref, w_ref, cos_ref, sin_ref, *refs):
    side_in, (k_ref, v_ref), side_out, (wbf_ref, acc0_ref, acc1_ref) = _split_refs(refs, n_side, 2)
    step = pl.program_id(0)
    _cast_weight_once(w_ref, wbf_ref, step)

    def compute():
        _cast_side(side_in, side_out)
        return _dot(x_ref[...], wbf_ref[...])

    def epilogue(acc_ref, rows):
        first_half, low_half = _head_masks((EPILOGUE_ROWS, LANES))
        acc = acc_ref[rows, :]
        k_cols = _rope(acc[:, :KV_WIDTH], cos_ref[rows, :], sin_ref[rows, :], first_half)
        v_cols = [acc[:, KV_WIDTH + c * LANES:KV_WIDTH + (c + 1) * LANES] for c in range(KV_WIDTH // LANES)]
        k_ref[rows, :] = jnp.concatenate(_duplicate_heads(k_cols, low_half), axis=1).astype(BF16)
        v_ref[rows, :] = jnp.concatenate(_duplicate_heads(v_cols, low_half), axis=1).astype(BF16)

    _deferred_epilogue(step, n_tiles, (acc0_ref, acc1_ref), compute, epilogue)


def _rope_table_spec(bm, tiles_per_seq):
    return pl.BlockSpec((bm, LANES), lambda s: (jnp.maximum(s - 1, 0) % tiles_per_seq, 0))


def _proj_q(x_bf, w_in, layer, cos_t, sin_t, bm, seq):
    t = x_bf.shape[0]
    n_tiles = t // bm
    return pl.pallas_call(
        functools.partial(_proj_q_kernel, n_tiles),
        grid=(n_tiles + 1,),
        in_specs=[
            pl.BlockSpec((bm, D_MODEL), _cur_tile(n_tiles)),
            _w_in_spec(layer, ATTN_WIDTH, OFF_Q // ATTN_WIDTH),
            _rope_table_spec(bm, seq // bm),
            _rope_table_spec(bm, seq // bm),
        ],
        out_specs=pl.BlockSpec((bm, 2 * ATTN_WIDTH), _done_tile),
        out_shape=jax.ShapeDtypeStruct((t, 2 * ATTN_WIDTH), BF16),
        scratch_shapes=[pltpu.VMEM((D_MODEL, ATTN_WIDTH), BF16)] + _acc_scratch(bm, ATTN_WIDTH),
        compiler_params=_params("arbitrary"),
        name="proj_q",
    )(x_bf, w_in, cos_t, sin_t)


def _proj_kv(x_bf, w_in, layer, cos_t, sin_t, side_weights, bm, seq):
    t = x_bf.shape[0]
    n_tiles = t // bm
    side_in, side_out, side_shapes = _side_cast_specs(side_weights, layer, n_tiles,
                                                      lambda s: jnp.minimum(s, n_tiles - 1))
    return pl.pallas_call(
        functools.partial(_proj_kv_kernel, n_tiles, len(side_weights)),
        grid=(n_tiles + 1,),
        in_specs=[
            pl.BlockSpec((bm, D_MODEL), _cur_tile(n_tiles)),
            _w_in_spec(layer, 2 * KV_WIDTH, OFF_K // (2 * KV_WIDTH)),
            _rope_table_spec(bm, seq // bm),
            _rope_table_spec(bm, seq // bm),
        ] + side_in,
        out_specs=[
            pl.BlockSpec((bm, 2 * KV_WIDTH), _done_tile),
            pl.BlockSpec((bm, 2 * KV_WIDTH), _done_tile),
        ] + side_out,
        out_shape=[
            jax.ShapeDtypeStruct((t, 2 * KV_WIDTH), BF16),
            jax.ShapeDtypeStruct((t, 2 * KV_WIDTH), BF16),
        ] + side_shapes,
        scratch_shapes=[pltpu.VMEM((D_MODEL, 2 * KV_WIDTH), BF16)] + _acc_scratch(bm, 2 * KV_WIDTH),
        compiler_params=_params("arbitrary"),
        name="proj_kv",
    )(x_bf, w_in, cos_t, sin_t, *[w for w, _ in side_weights])


def _swa_kernel(sink_ref, q_ref, kp_ref, kc_ref, vp_ref, vc_ref, o_ref):
    n = pl.program_id(1)
    pair_rows = 2 * WINDOW
    lane = lax.broadcasted_iota(jnp.int32, (WINDOW, LANES), 1)
    low_half = lane < HEAD_DIM
    row = lax.broadcasted_iota(jnp.int32, (pair_rows, WINDOW), 0)
    key = lax.broadcasted_iota(jnp.int32, (pair_rows, WINDOW), 1)
    in_own_block = key <= (row % WINDOW)
    top_rows = lax.broadcasted_iota(jnp.int32, (pair_rows, 1), 0) < WINDOW
    first_key = jnp.where(n > 0, 0, WINDOW)
    first_bias = jnp.where(lax.broadcasted_iota(jnp.int32, (1, WINDOW), 1) >= first_key, 0.0, -jnp.inf)
    for blk in range(SWA_BLOCKS):
        own = slice(blk * WINDOW, (blk + 1) * WINDOW)
        outs = []
        for p in range(N_Q_HEADS // 2):
            g = (2 * p) // (N_Q_HEADS // N_KV_HEADS)
            kv_cols = slice(g * LANES, (g + 1) * LANES)
            if blk == 0:
                k_prev, v_prev = kp_ref[:, kv_cols], vp_ref[:, kv_cols]
            else:
                before = slice((blk - 1) * WINDOW, blk * WINDOW)
                k_prev, v_prev = kc_ref[before, kv_cols], vc_ref[before, kv_cols]
            qs = jnp.concatenate([q_ref[own, 2 * p * LANES:(2 * p + 1) * LANES],
                                  q_ref[own, (2 * p + 1) * LANES:(2 * p + 2) * LANES]], axis=0)
            s_prev = _dot_nt(qs, k_prev)
            if blk == 0:
                s_prev = s_prev + first_bias
            s = jnp.where(in_own_block, _dot_nt(qs, kc_ref[own, kv_cols]), s_prev)
            sink = jnp.where(top_rows, sink_ref[2 * p], sink_ref[2 * p + 1])
            m = jnp.maximum(jnp.max(s, axis=-1, keepdims=True), sink)
            e = jnp.exp(s - m)
            denom = jnp.sum(e, axis=-1, keepdims=True) + jnp.exp(sink - m)
            e_own = jnp.where(in_own_block, e, 0.0).astype(BF16)
            e_prev = jnp.where(in_own_block, 0.0, e).astype(BF16)
            o2 = (_dot(e_own, vc_ref[own, kv_cols]) + _dot(e_prev, v_prev)) / denom
            outs.append(jnp.where(low_half, o2[:WINDOW], o2[WINDOW:]))
        o_ref[own, :] = jnp.concatenate(outs, axis=1).astype(BF16)


def _swa(q, k2, v2, sinks, bsz, seq):
    steps_per_seq = seq // (SWA_BLOCKS * WINDOW)
    cur = lambda b, n: (b * steps_per_seq + n, 0)
    prev = lambda b, n: (SWA_BLOCKS * (b * steps_per_seq + n) - jnp.minimum(n, 1), 0)
    return pl.pallas_call(
        _swa_kernel,
        grid=(bsz, steps_per_seq),
        in_specs=[
            pl.BlockSpec(memory_space=pltpu.SMEM),
            pl.BlockSpec((SWA_BLOCKS * WINDOW, 2 * ATTN_WIDTH), cur),
            pl.BlockSpec((WINDOW, 2 * KV_WIDTH), prev),
            pl.BlockSpec((SWA_BLOCKS * WINDOW, 2 * KV_WIDTH), cur),
            pl.BlockSpec((WINDOW, 2 * KV_WIDTH), prev),
            pl.BlockSpec((SWA_BLOCKS * WINDOW, 2 * KV_WIDTH), cur),
        ],
        out_specs=pl.BlockSpec((SWA_BLOCKS * WINDOW, ATTN_WIDTH), cur),
        out_shape=jax.ShapeDtypeStruct((bsz * seq, ATTN_WIDTH), BF16),
        compiler_params=_params("parallel", "parallel"),
        name="swa",
    )(sinks, q, k2, k2, v2, v2)


def _gate_branch_kernel(xb_ref, gu_ref, vn_ref, at_ref, ws_ref, bs_ref, bg_ref, wa_ref, wb_ref, wg_ref,
                        o_ref, gated_ref):
    bm = gu_ref.shape[0]
    t_idx = lax.broadcasted_iota(jnp.int32, (CHUNK, CHUNK), 0)
    s_idx = lax.broadcasted_iota(jnp.int32, (CHUNK, CHUNK), 1)
    causal = s_idx <= t_idx
    bs = bs_ref[...]
    for g in range(GMLP_GROUPS):
        w = jnp.where(causal, ws_ref[g], 0.0).astype(BF16)
        bias = bs[:, g:g + 1]
        cols = slice(g * LANES, (g + 1) * LANES)
        for c in range(bm // CHUNK):
            rows = slice(c * CHUNK, (c + 1) * CHUNK)
            mixed = _dot(w, vn_ref[rows, cols]) + bias
            gated_ref[rows, cols] = (gu_ref[rows, cols] * mixed).astype(BF16)
    for c in range(D_MODEL // MERGE_COLS):
        cols_a = slice(c * MERGE_COLS, (c + 1) * MERGE_COLS)
        cols_b = slice(D_MODEL + c * MERGE_COLS, D_MODEL + (c + 1) * MERGE_COLS)
        ya = _dot(gated_ref[...], wa_ref[:, cols_a])
        yb = _dot(at_ref[...], wb_ref[:, cols_a])
        ga = jax.nn.sigmoid(_dot(xb_ref[...], wg_ref[:, cols_a]) + bg_ref[:, cols_a])
        gb = jax.nn.sigmoid(_dot(xb_ref[...], wg_ref[:, cols_b]) + bg_ref[:, cols_b])
        o_ref[:, cols_a] = (ga * ya + gb * yb).astype(BF16)


def _gate_branch(x_bf, gu, vn, attn, w_s, b_s_t, b_gate, w_a, w_b, w_g, bm):
    t = gu.shape[0]
    row = lambda i: (i, 0)
    const2 = lambda i: (0, 0)
    return pl.pallas_call(
        _gate_branch_kernel,
        grid=(t // bm,),
        in_specs=[
            pl.BlockSpec((bm, D_MODEL), row),
            pl.BlockSpec((bm, GMLP_WIDTH), row),
            pl.BlockSpec((bm, GMLP_WIDTH), row),
            pl.BlockSpec((bm, ATTN_WIDTH), row),
            pl.BlockSpec((GMLP_GROUPS, CHUNK, CHUNK), lambda i: (0, 0, 0)),
            pl.BlockSpec((CHUNK, GMLP_GROUPS), const2),
            pl.BlockSpec((1, GATE_WIDTH), const2),
            pl.BlockSpec((GMLP_WIDTH, D_MODEL), const2),
            pl.BlockSpec((ATTN_WIDTH, D_MODEL), const2),
            pl.BlockSpec((D_MODEL, GATE_WIDTH), const2),
        ],
        out_specs=pl.BlockSpec((bm, D_MODEL), row),
        out_shape=jax.ShapeDtypeStruct((t, D_MODEL), BF16),
        scratch_shapes=[pltpu.VMEM((bm, GMLP_WIDTH), BF16)],
        compiler_params=_params("parallel"),
        name="gate_branch",
    )(x_bf, gu, vn, attn, w_s, b_s_t, b_gate, w_a, w_b, w_g)


def _post_norm(g_ref, b_ref, o_ref, obf_ref):
    def epilogue(z_ref, rows):
        y = _layer_norm(z_ref[rows, :], g_ref[...], b_ref[...])
        o_ref[rows, :] = y
        obf_ref[rows, :] = y.astype(BF16)
    return epilogue


def _mem_kv_kernel(m_ref, w_ref, o_ref):
    o_ref[...] = _dot(m_ref[...].astype(BF16), w_ref[...].astype(BF16)).astype(BF16)


def _mem_kv(mem2d, w_xkv, layer):
    rows = mem2d.shape[0]
    return pl.pallas_call(
        _mem_kv_kernel,
        grid=(2,),
        in_specs=[
            pl.BlockSpec((rows, D_MODEL), lambda j: (0, 0)),
            pl.BlockSpec((None, D_MODEL, X_WIDTH), lambda j: (layer, 0, j)),
        ],
        out_specs=pl.BlockSpec((rows, X_WIDTH), lambda j: (0, j)),
        out_shape=jax.ShapeDtypeStruct((rows, 2 * X_WIDTH), BF16),
        compiler_params=_params("parallel"),
        name="mem_kv",
    )(mem2d, w_xkv)


def _post_mixer_kernel(n_tiles, m_ref, x_ref, k_ref, v_ref, wo_ref, g1_ref, b1_ref, wq_ref, wxo_ref, g2_ref, b2_ref,
                       o_ref, obf_ref, z0_ref, z1_ref):
    def compute_z():
        z1 = ALPHA * x_ref[...] + _dot(m_ref[...], wo_ref[...])
        x1 = _layer_norm(z1, g1_ref[...], b1_ref[...])
        q = _dot(x1.astype(BF16), wq_ref[...]).astype(BF16)
        scale = X_HEAD_DIM ** -0.5
        outs = []
        for h in range(X_HEADS):
            cols = slice(h * X_HEAD_DIM, (h + 1) * X_HEAD_DIM)
            s = _dot_nt(q[:, cols], k_ref[:, cols]) * scale
            m = jnp.max(s, axis=-1, keepdims=True)
            p = jnp.exp(s - m)
            denom = jnp.sum(p, axis=-1, keepdims=True)
            outs.append(_dot(p.astype(BF16), v_ref[:, cols]) / denom)
        o = jnp.concatenate(outs, axis=1).astype(BF16)
        return ALPHA * x1 + _dot(o, wxo_ref[...])

    _deferred_epilogue(pl.program_id(0), n_tiles, (z0_ref, z1_ref), compute_z,
                       _post_norm(g2_ref, b2_ref, o_ref, obf_ref))


def _post_mixer(merged, x, kv, w_o, ln1_g, ln1_b, w_xq, w_xo, ln2_g, ln2_b, bm, seq):
    t = x.shape[0]
    n_tiles = t // bm
    tiles_per_seq = seq // bm
    batch = lambda s: jnp.minimum(s, n_tiles - 1) // tiles_per_seq
    const2 = lambda s: (0, 0)
    return pl.pallas_call(
        functools.partial(_post_mixer_kernel, n_tiles),
        grid=(n_tiles + 1,),
        in_specs=[
            pl.BlockSpec((bm, D_MODEL), _cur_tile(n_tiles)),
            pl.BlockSpec((bm, D_MODEL), _cur_tile(n_tiles)),
            pl.BlockSpec((MEM_LEN, X_WIDTH), lambda s: (batch(s), 0)),
            pl.BlockSpec((MEM_LEN, X_WIDTH), lambda s: (batch(s), 1)),
            pl.BlockSpec((D_MODEL, D_MODEL), const2),
            pl.BlockSpec((1, D_MODEL), const2),
            pl.BlockSpec((1, D_MODEL), const2),
            pl.BlockSpec((D_MODEL, X_WIDTH), const2),
            pl.BlockSpec((X_WIDTH, D_MODEL), const2),
            pl.BlockSpec((1, D_MODEL), const2),
            pl.BlockSpec((1, D_MODEL), const2),
        ],
        out_specs=[pl.BlockSpec((bm, D_MODEL), _done_tile), pl.BlockSpec((bm, D_MODEL), _done_tile)],
        out_shape=[jax.ShapeDtypeStruct((t, D_MODEL), F32), jax.ShapeDtypeStruct((t, D_MODEL), BF16)],
        scratch_shapes=_acc_scratch(bm, D_MODEL),
        compiler_params=_params("arbitrary"),
        name="post_mixer",
    )(merged, x, kv, kv, w_o, ln1_g, ln1_b, w_xq, w_xo, ln2_g, ln2_b)


def _mlp_kernel(xbf_ref, x_ref, wu_ref, wd_ref, g_ref, b_ref, o_ref, obf_ref, acc_ref):
    f = pl.program_id(1)

    @pl.when(f == 0)
    def _():
        acc_ref[...] = jnp.zeros_like(acc_ref)

    h = jnp.maximum(_dot(xbf_ref[...], wu_ref[...]), 0.0)
    acc_ref[...] += _dot((h * h).astype(BF16), wd_ref[...])

    @pl.when(f == pl.num_programs(1) - 1)
    def _():
        z = ALPHA * x_ref[...] + acc_ref[...]
        y = _layer_norm(z, g_ref[...], b_ref[...])
        o_ref[...] = y
        obf_ref[...] = y.astype(BF16)


def _mlp(x_bf, x, w_up, w_down, ln_g, ln_b, bm, bf):
    t = x.shape[0]
    n_f = D_FF // bf
    row = lambda i, f: (i, 0)
    const2 = lambda i, f: (0, 0)
    late_row = lambda i, f: (jnp.where(f >= n_f // 2, i, jnp.maximum(i - 1, 0)), 0)
    return pl.pallas_call(
        _mlp_kernel,
        grid=(t // bm, n_f),
        in_specs=[
            pl.BlockSpec((bm, D_MODEL), row),
            pl.BlockSpec((bm, D_MODEL), late_row),
            pl.BlockSpec((D_MODEL, bf), lambda i, f: (0, f)),
            pl.BlockSpec((bf, D_MODEL), lambda i, f: (f, 0)),
            pl.BlockSpec((1, D_MODEL), const2),
            pl.BlockSpec((1, D_MODEL), const2),
        ],
        out_specs=[pl.BlockSpec((bm, D_MODEL), row), pl.BlockSpec((bm, D_MODEL), row)],
        out_shape=[jax.ShapeDtypeStruct((t, D_MODEL), F32), jax.ShapeDtypeStruct((t, D_MODEL), BF16)],
        scratch_shapes=[pltpu.VMEM((bm, D_MODEL), F32)],
        compiler_params=_params("parallel", "arbitrary"),
        name="mlp",
    )(x_bf, x, w_up, w_down, ln_g, ln_b)


def _rope_tables(seq):
    inv = 1.0 / (ROPE_THETA ** (jnp.arange(0, HEAD_DIM, 2, dtype=F32) / HEAD_DIM))
    ang = jnp.arange(seq, dtype=F32)[:, None] * inv[None, :]
    cos, sin = jnp.cos(ang), jnp.sin(ang)
    reps = LANES // HEAD_DIM
    cos_t = jnp.tile(jnp.concatenate([cos, cos], axis=1), (1, reps))
    sin_t = jnp.tile(jnp.concatenate([-sin, sin], axis=1), (1, reps))
    return cos_t, sin_t


def kernel(x, mem, w_in, b_gate, ln_v_g, ln_v_b, w_s, b_s, sinks, w_br_a, w_br_b, w_o, ln1_g, ln1_b,
           w_xq, w_xkv, w_xo, ln2_g, ln2_b, w_up, w_down, ln3_g, ln3_b):
    bsz, seq, d = x.shape
    t = bsz * seq
    cos_t, sin_t = _rope_tables(seq)
    xf = x.reshape(t, d)
    mem2d = mem.reshape(bsz * MEM_LEN, d)
    row = lambda a: a.reshape(1, -1)
    xb = xf
    for l in range(DEPTH):
        out = _proj_u(xb, w_in, l, [(w_down, 0)], bm=512 if l == 0 else 1024)
        if l == 0:
            gu, xb, w_down_bf = out
        else:
            gu, w_down_bf = out
        vn, w_up_bf = _proj_v(xb, w_in, l, row(ln_v_g[l]), row(ln_v_b[l]), [(w_up, 0)], bm=1024)
        q = _proj_q(xb, w_in, l, cos_t, sin_t, bm=1024, seq=seq)
        k2, v2, w_g, w_a, w_b, w_o_bf = _proj_kv(
            xb, w_in, l, cos_t, sin_t, [(w_in, OFF_GATE), (w_br_a, 0), (w_br_b, 0), (w_o, 0)], bm=1024, seq=seq)
        attn = _swa(q, k2, v2, sinks[l], bsz, seq)
        merged = _gate_branch(xb, gu, vn, attn, w_s[l], b_s[l].T, row(b_gate[l]), w_a, w_b, w_g, bm=512)
        kv = _mem_kv(mem2d, w_xkv, l)
        xf, xb = _post_mixer(merged, xf, kv, w_o_bf, row(ln1_g[l]), row(ln1_b[l]), w_xq[l].astype(BF16),
                             w_xo[l].astype(BF16), row(ln2_g[l]), row(ln2_b[l]), bm=512, seq=seq)
        xf, xb = _mlp(xb, xf, w_up_bf, w_down_bf, row(ln3_g[l]), row(ln3_b[l]), bm=512, bf=1024)
    return xf.reshape(bsz, seq, d)
```

```python
import functools

import jax
import jax.numpy as jnp
from jax import lax
from jax.experimental import pallas as pl
from jax.experimental.pallas import tpu as pltpu

D_MODEL = 2048
DEPTH = 2
MEM_LEN = 256
CHUNK = 128
GMLP_WIDTH = 1024
GMLP_GROUPS = 8
N_Q_HEADS = 16
N_KV_HEADS = 4
HEAD_DIM = 64
ATTN_WIDTH = N_Q_HEADS * HEAD_DIM
KV_WIDTH = N_KV_HEADS * HEAD_DIM
WINDOW = 128
ROPE_THETA = 10000.0
X_HEADS = 4
X_HEAD_DIM = 128
X_WIDTH = X_HEADS * X_HEAD_DIM
D_FF = 4 * D_MODEL
LN_EPS = 1e-5
ALPHA = (2 * DEPTH) ** 0.25

OFF_V = GMLP_WIDTH
OFF_Q = 2 * GMLP_WIDTH
OFF_K = OFF_Q + ATTN_WIDTH
OFF_GATE = OFF_K + 2 * KV_WIDTH
GATE_WIDTH = 2 * D_MODEL

LANES = 128
VMEM_LIMIT_BYTES = 56 * 1024 * 1024
SWA_BLOCKS = 4
EPILOGUE_ROWS = 32
SCORE_SCALE = HEAD_DIM ** -0.5

BF16 = jnp.bfloat16
F32 = jnp.float32


def _params(*semantics):
    return pltpu.CompilerParams(dimension_semantics=semantics, vmem_limit_bytes=VMEM_LIMIT_BYTES)


def _dot(a, b):
    return jnp.dot(a, b, preferred_element_type=F32)


def _dot_nt(a, b):
    return lax.dot_general(a, b, (((1,), (1,)), ((), ())), preferred_element_type=F32)


def _layer_norm(z, g, b):
    mu = jnp.mean(z, axis=-1, keepdims=True)
    zc = z - mu
    var = jnp.mean(zc * zc, axis=-1, keepdims=True)
    return zc * lax.rsqrt(var + LN_EPS) * g + b


def _gelu(x):
    return jax.nn.gelu(x, approximate=True)


def _cast_weight_once(w_ref, wbf_ref, step):
    @pl.when(step == 0)
    def _():
        wbf_ref[...] = w_ref[...].astype(BF16)


def _deferred_epilogue(step, n_tiles, acc_refs, compute, epilogue):
    def run_epilogue(acc_ref):
        for r in range(0, acc_ref.shape[0], EPILOGUE_ROWS):
            epilogue(acc_ref, slice(r, r + EPILOGUE_ROWS))

    @pl.when(step == 0)
    def _():
        acc_refs[0][...] = compute()

    for parity in range(2):
        @pl.when((step > 0) & (step < n_tiles) & (step % 2 == parity))
        def _():
            run_epilogue(acc_refs[1 - parity])
            acc_refs[parity][...] = compute()

    @pl.when(step == n_tiles)
    def _():
        run_epilogue(acc_refs[1 - n_tiles % 2])


def _cur_tile(n_tiles):
    return lambda s: (jnp.minimum(s, n_tiles - 1), 0)


def _done_tile(s):
    return (jnp.maximum(s - 1, 0), 0)


def _acc_scratch(bm, width):
    return [pltpu.VMEM((bm, width), F32), pltpu.VMEM((bm, width), F32)]


def _w_in_spec(layer, width, col_block):
    return pl.BlockSpec((None, D_MODEL, width), lambda s: (layer, 0, col_block))


def _proj_u_kernel(x_ref, w_ref, gu_ref, wbf_ref):
    _cast_weight_once(w_ref, wbf_ref, pl.program_id(0))
    gu_ref[...] = _gelu(_dot(x_ref[...], wbf_ref[...]))


def _proj_v_kernel(x_ref, w_ref, g_ref, b_ref, vn_ref, wbf_ref):
    _cast_weight_once(w_ref, wbf_ref, pl.program_id(0))
    v = _gelu(_dot(x_ref[...], wbf_ref[...]))
    vn_ref[...] = _layer_norm(v, g_ref[...], b_ref[...]).astype(BF16)


def _proj_u_from_f32_kernel(x_ref, w_ref, gu_ref, xb_ref, wbf_ref):
    _cast_weight_once(w_ref, wbf_ref, pl.program_id(0))
    xb = x_ref[...].astype(BF16)
    xb_ref[...] = xb
    gu_ref[...] = _gelu(_dot(xb, wbf_ref[...]))


def _proj_u(x_bf, w_in, layer, bm):
    t = x_bf.shape[0]
    return pl.pallas_call(
        _proj_u_kernel,
        grid=(t // bm,),
        in_specs=[
            pl.BlockSpec((bm, D_MODEL), lambda i: (i, 0)),
            _w_in_spec(layer, GMLP_WIDTH, 0),
        ],
        out_specs=pl.BlockSpec((bm, GMLP_WIDTH), lambda i: (i, 0)),
        out_shape=jax.ShapeDtypeStruct((t, GMLP_WIDTH), F32),
        scratch_shapes=[pltpu.VMEM((D_MODEL, GMLP_WIDTH), BF16)],
        compiler_params=_params("arbitrary"),
        name="proj_u",
    )(x_bf, w_in)


def _proj_u_from_f32(x, w_in, layer, bm):
    t = x.shape[0]
    return pl.pallas_call(
        _proj_u_from_f32_kernel,
        grid=(t // bm,),
        in_specs=[
            pl.BlockSpec((bm, D_MODEL), lambda i: (i, 0)),
            _w_in_spec(layer, GMLP_WIDTH, 0),
        ],
        out_specs=[pl.BlockSpec((bm, GMLP_WIDTH), lambda i: (i, 0)), pl.BlockSpec((bm, D_MODEL), lambda i: (i, 0))],
        out_shape=[jax.ShapeDtypeStruct((t, GMLP_WIDTH), F32), jax.ShapeDtypeStruct((t, D_MODEL), BF16)],
        scratch_shapes=[pltpu.VMEM((D_MODEL, GMLP_WIDTH), BF16)],
        compiler_params=_params("arbitrary"),
        name="proj_u_f32",
    )(x, w_in)


def _proj_v(x_bf, w_in, layer, ln_g, ln_b, bm):
    t = x_bf.shape[0]
    return pl.pallas_call(
        _proj_v_kernel,
        grid=(t // bm,),
        in_specs=[
            pl.BlockSpec((bm, D_MODEL), lambda i: (i, 0)),
            _w_in_spec(layer, GMLP_WIDTH, OFF_V // GMLP_WIDTH),
            pl.BlockSpec((1, GMLP_WIDTH), lambda i: (0, 0)),
            pl.BlockSpec((1, GMLP_WIDTH), lambda i: (0, 0)),
        ],
        out_specs=pl.BlockSpec((bm, GMLP_WIDTH), lambda i: (i, 0)),
        out_shape=jax.ShapeDtypeStruct((t, GMLP_WIDTH), BF16),
        scratch_shapes=[pltpu.VMEM((D_MODEL, GMLP_WIDTH), BF16)],
        compiler_params=_params("arbitrary"),
        name="proj_v",
    )(x_bf, w_in, ln_g, ln_b)


def _rope(x, cos, sin_signed, first_half):
    outs = []
    for c in range(x.shape[1] // LANES):
        xs = x[:, c * LANES:(c + 1) * LANES]
        swapped = jnp.where(first_half, pltpu.roll(xs, LANES - HEAD_DIM // 2, 1),
                            pltpu.roll(xs, HEAD_DIM // 2, 1))
        outs.append(xs * cos + swapped * sin_signed)
    return outs


def _duplicate_heads(cols, low_half):
    outs = []
    for xs in cols:
        other = pltpu.roll(xs, HEAD_DIM, 1)
        outs.append(jnp.where(low_half, xs, other))
        outs.append(jnp.where(low_half, other, xs))
    return outs


def _head_masks(shape):
    lane = lax.broadcasted_iota(jnp.int32, shape, 1)
    return (lane % HEAD_DIM) < (HEAD_DIM // 2), (lane % LANES) < HEAD_DIM


def _proj_q_kernel(n_tiles, x_ref, w_ref, cos_ref, sin_ref, q_ref, wbf_ref, acc0_ref, acc1_ref):
    step = pl.program_id(0)
    _cast_weight_once(w_ref, wbf_ref, step)

    def epilogue(acc_ref, rows):
        first_half, low_half = _head_masks((EPILOGUE_ROWS, LANES))
        cols = _rope(acc_ref[rows, :], cos_ref[rows, :] * SCORE_SCALE, sin_ref[rows, :] * SCORE_SCALE, first_half)
        padded = []
        for c in cols:
            padded.append(jnp.where(low_half, c, 0.0))
            padded.append(jnp.where(low_half, 0.0, c))
        q_ref[rows, :] = jnp.concatenate(padded, axis=1).astype(BF16)

    _deferred_epilogue(step, n_tiles, (acc0_ref, acc1_ref), lambda: _dot(x_ref[...], wbf_ref[...]), epilogue)


def _proj_kv_kernel(n_tiles, x_ref, w_ref, cos_ref, sin_ref, k_ref, v_ref, wbf_ref, acc0_ref, acc1_ref):
    step = pl.program_id(0)
    _cast_weight_once(w_ref, wbf_ref, step)

    def epilogue(acc_ref, rows):
        first_half, low_half = _head_masks((EPILOGUE_ROWS, LANES))
        acc = acc_ref[rows, :]
        k_cols = _rope(acc[:, :KV_WIDTH], cos_ref[rows, :], sin_ref[rows, :], first_half)
        v_cols = [acc[:, KV_WIDTH + c * LANES:KV_WIDTH + (c + 1) * LANES] for c in range(KV_WIDTH // LANES)]
        k_ref[rows, :] = jnp.concatenate(_duplicate_heads(k_cols, low_half), axis=1).astype(BF16)
        v_ref[rows, :] = jnp.concatenate(_duplicate_heads(v_cols, low_half), axis=1).astype(BF16)

    _deferred_epilogue(step, n_tiles, (acc0_ref, acc1_ref), lambda: _dot(x_ref[...], wbf_ref[...]), epilogue)


def _rope_table_spec(bm, tiles_per_seq):
    return pl.BlockSpec((bm, LANES), lambda s: (jnp.maximum(s - 1, 0) % tiles_per_seq, 0))


def _proj_q(x_bf, w_in, layer, cos_t, sin_t, bm, seq):
    t = x_bf.shape[0]
    n_tiles = t // bm
    return pl.pallas_call(
        functools.partial(_proj_q_kernel, n_tiles),
        grid=(n_tiles + 1,),
        in_specs=[
            pl.BlockSpec((bm, D_MODEL), _cur_tile(n_tiles)),
            _w_in_spec(layer, ATTN_WIDTH, OFF_Q // ATTN_WIDTH),
            _rope_table_spec(bm, seq // bm),
            _rope_table_spec(bm, seq // bm),
        ],
        out_specs=pl.BlockSpec((bm, 2 * ATTN_WIDTH), _done_tile),
        out_shape=jax.ShapeDtypeStruct((t, 2 * ATTN_WIDTH), BF16),
        scratch_shapes=[pltpu.VMEM((D_MODEL, ATTN_WIDTH), BF16)] + _acc_scratch(bm, ATTN_WIDTH),
        compiler_params=_params("arbitrary"),
        name="proj_q",
    )(x_bf, w_in, cos_t, sin_t)


def _proj_kv(x_bf, w_in, layer, cos_t, sin_t, bm, seq):
    t = x_bf.shape[0]
    n_tiles = t // bm
    return pl.pallas_call(
        functools.partial(_proj_kv_kernel, n_tiles),
        grid=(n_tiles + 1,),
        in_specs=[
            pl.BlockSpec((bm, D_MODEL), _cur_tile(n_tiles)),
            _w_in_spec(layer, 2 * KV_WIDTH, OFF_K // (2 * KV_WIDTH)),
            _rope_table_spec(bm, seq // bm),
            _rope_table_spec(bm, seq // bm),
        ],
        out_specs=[
            pl.BlockSpec((bm, 2 * KV_WIDTH), _done_tile),
            pl.BlockSpec((bm, 2 * KV_WIDTH), _done_tile),
        ],
        out_shape=[
            jax.ShapeDtypeStruct((t, 2 * KV_WIDTH), BF16),
            jax.ShapeDtypeStruct((t, 2 * KV_WIDTH), BF16),
        ],
        scratch_shapes=[pltpu.VMEM((D_MODEL, 2 * KV_WIDTH), BF16)] + _acc_scratch(bm, 2 * KV_WIDTH),
        compiler_params=_params("arbitrary"),
        name="proj_kv",
    )(x_bf, w_in, cos_t, sin_t)


def _proj_gate_kernel(n_tiles, m_tiles, n_side, x_ref, w0_ref, w1_ref, b_ref, *refs):
    side_in, o_ref, side_out = refs[:n_side], refs[n_side], refs[n_side + 1:2 * n_side + 1]
    wbf_ref, acc0_ref, acc1_ref = refs[2 * n_side + 1:]
    step = pl.program_id(0)
    half = w0_ref.shape[1]

    @pl.when((step % m_tiles == 0) & (step < n_tiles))
    def _():
        wbf_ref[:, :half] = w0_ref[...].astype(BF16)
        wbf_ref[:, half:] = w1_ref[...].astype(BF16)

    def compute():
        for src, dst in zip(side_in, side_out):
            dst[...] = src[...].astype(BF16)
        return _dot(x_ref[...], wbf_ref[...])

    def epilogue(acc_ref, rows):
        o_ref[rows, :] = jax.nn.sigmoid(acc_ref[rows, :] + b_ref[...])

    _deferred_epilogue(step, n_tiles, (acc0_ref, acc1_ref), compute, epilogue)


def _proj_gate(x_bf, w_in, layer, b_gate, side_weights, bm, bn):
    t = x_bf.shape[0]
    m_tiles = t // bm
    n_tiles = m_tiles * (GATE_WIDTH // bn)
    half = bn // 2
    first = OFF_GATE // half
    cur = lambda s: jnp.minimum(s, n_tiles - 1)
    done = lambda s: jnp.maximum(s - 1, 0)
    slab = lambda w: (w.shape[1] // n_tiles, w.shape[2])
    return pl.pallas_call(
        functools.partial(_proj_gate_kernel, n_tiles, m_tiles, len(side_weights)),
        grid=(n_tiles + 1,),
        in_specs=[
            pl.BlockSpec((bm, D_MODEL), lambda s: (cur(s) % m_tiles, 0)),
            pl.BlockSpec((None, D_MODEL, half), lambda s: (layer, 0, first + 2 * (cur(s) // m_tiles))),
            pl.BlockSpec((None, D_MODEL, half), lambda s: (layer, 0, first + 2 * (cur(s) // m_tiles) + 1)),
            pl.BlockSpec((1, bn), lambda s: (0, done(s) // m_tiles)),
        ] + [pl.BlockSpec((None,) + slab(w), lambda s: (layer, cur(s), 0)) for w in side_weights],
        out_specs=[pl.BlockSpec((bm, bn), lambda s: (done(s) % m_tiles, done(s) // m_tiles))]
        + [pl.BlockSpec(slab(w), lambda s: (cur(s), 0)) for w in side_weights],
        out_shape=[jax.ShapeDtypeStruct((t, GATE_WIDTH), F32)]
        + [jax.ShapeDtypeStruct(w.shape[1:], BF16) for w in side_weights],
        scratch_shapes=[pltpu.VMEM((D_MODEL, bn), BF16)] + _acc_scratch(bm, bn),
        compiler_params=_params("arbitrary"),
        name="proj_gate",
    )(x_bf, w_in, w_in, b_gate, *side_weights)


def _swa_kernel(sink_ref, q_ref, kp_ref, kc_ref, vp_ref, vc_ref, o_ref):
    n = pl.program_id(1)
    pair_rows = 2 * WINDOW
    lane = lax.broadcasted_iota(jnp.int32, (WINDOW, LANES), 1)
    low_half = lane < HEAD_DIM
    row = lax.broadcasted_iota(jnp.int32, (pair_rows, WINDOW), 0)
    key = lax.broadcasted_iota(jnp.int32, (pair_rows, WINDOW), 1)
    in_own_block = key <= (row % WINDOW)
    top_rows = lax.broadcasted_iota(jnp.int32, (pair_rows, 1), 0) < WINDOW
    first_key = jnp.where(n > 0, 0, WINDOW)
    first_bias = jnp.where(lax.broadcasted_iota(jnp.int32, (1, WINDOW), 1) >= first_key, 0.0, -jnp.inf)
    for blk in range(SWA_BLOCKS):
        own = slice(blk * WINDOW, (blk + 1) * WINDOW)
        outs = []
        for p in range(N_Q_HEADS // 2):
            g = (2 * p) // (N_Q_HEADS // N_KV_HEADS)
            kv_cols = slice(g * LANES, (g + 1) * LANES)
            if blk == 0:
                k_prev, v_prev = kp_ref[:, kv_cols], vp_ref[:, kv_cols]
            else:
                before = slice((blk - 1) * WINDOW, blk * WINDOW)
                k_prev, v_prev = kc_ref[before, kv_cols], vc_ref[before, kv_cols]
            qs = jnp.concatenate([q_ref[own, 2 * p * LANES:(2 * p + 1) * LANES],
                                  q_ref[own, (2 * p + 1) * LANES:(2 * p + 2) * LANES]], axis=0)
            s_prev = _dot_nt(qs, k_prev)
            if blk == 0:
                s_prev = s_prev + first_bias
            s = jnp.where(in_own_block, _dot_nt(qs, kc_ref[own, kv_cols]), s_prev)
            sink = jnp.where(top_rows, sink_ref[2 * p], sink_ref[2 * p + 1])
            m = jnp.maximum(jnp.max(s, axis=-1, keepdims=True), sink)
            e = jnp.exp(s - m)
            denom = jnp.sum(e, axis=-1, keepdims=True) + jnp.exp(sink - m)
            e_own = jnp.where(in_own_block, e, 0.0).astype(BF16)
            e_prev = jnp.where(in_own_block, 0.0, e).astype(BF16)
            o2 = (_dot(e_own, vc_ref[own, kv_cols]) + _dot(e_prev, v_prev)) / denom
            outs.append(jnp.where(low_half, o2[:WINDOW], o2[WINDOW:]))
        o_ref[own, :] = jnp.concatenate(outs, axis=1).astype(BF16)


def _swa(q, k2, v2, sinks, bsz, seq):
    steps_per_seq = seq // (SWA_BLOCKS * WINDOW)
    cur = lambda b, n: (b * steps_per_seq + n, 0)
    prev = lambda b, n: (SWA_BLOCKS * (b * steps_per_seq + n) - jnp.minimum(n, 1), 0)
    return pl.pallas_call(
        _swa_kernel,
        grid=(bsz, steps_per_seq),
        in_specs=[
            pl.BlockSpec(memory_space=pltpu.SMEM),
            pl.BlockSpec((SWA_BLOCKS * WINDOW, 2 * ATTN_WIDTH), cur),
            pl.BlockSpec((WINDOW, 2 * KV_WIDTH), prev),
            pl.BlockSpec((SWA_BLOCKS * WINDOW, 2 * KV_WIDTH), cur),
            pl.BlockSpec((WINDOW, 2 * KV_WIDTH), prev),
            pl.BlockSpec((SWA_BLOCKS * WINDOW, 2 * KV_WIDTH), cur),
        ],
        out_specs=pl.BlockSpec((SWA_BLOCKS * WINDOW, ATTN_WIDTH), cur),
        out_shape=jax.ShapeDtypeStruct((bsz * seq, ATTN_WIDTH), BF16),
        compiler_params=_params("parallel", "parallel"),
        name="swa",
    )(sinks, q, k2, k2, v2, v2)


def _branch_kernel(gu_ref, vn_ref, at_ref, ga_ref, gb_ref, ws_ref, bs_ref, wa_ref, wb_ref, o_ref, gated_ref):
    bm = gu_ref.shape[0]
    t_idx = lax.broadcasted_iota(jnp.int32, (CHUNK, CHUNK), 0)
    s_idx = lax.broadcasted_iota(jnp.int32, (CHUNK, CHUNK), 1)
    causal = s_idx <= t_idx
    bs = bs_ref[...]
    for g in range(GMLP_GROUPS):
        w = jnp.where(causal, ws_ref[g], 0.0).astype(BF16)
        bias = bs[:, g:g + 1]
        cols = slice(g * LANES, (g + 1) * LANES)
        for c in range(bm // CHUNK):
            rows = slice(c * CHUNK, (c + 1) * CHUNK)
            mixed = _dot(w, vn_ref[rows, cols]) + bias
            gated_ref[rows, cols] = (gu_ref[rows, cols] * mixed).astype(BF16)
    ya = _dot(gated_ref[...], wa_ref[...])
    yb = _dot(at_ref[...], wb_ref[...])
    o_ref[...] = (ga_ref[...] * ya + gb_ref[...] * yb).astype(BF16)


def _branch(gu, vn, attn, gates, w_s, b_s_t, w_a, w_b, bm):
    t = gu.shape[0]
    row = lambda i: (i, 0)
    const2 = lambda i: (0, 0)
    return pl.pallas_call(
        _branch_kernel,
        grid=(t // bm,),
        in_specs=[
            pl.BlockSpec((bm, GMLP_WIDTH), row),
            pl.BlockSpec((bm, GMLP_WIDTH), row),
            pl.BlockSpec((bm, ATTN_WIDTH), row),
            pl.BlockSpec((bm, D_MODEL), lambda i: (i, 0)),
            pl.BlockSpec((bm, D_MODEL), lambda i: (i, 1)),
            pl.BlockSpec((GMLP_GROUPS, CHUNK, CHUNK), lambda i: (0, 0, 0)),
            pl.BlockSpec((CHUNK, GMLP_GROUPS), const2),
            pl.BlockSpec((GMLP_WIDTH, D_MODEL), const2),
            pl.BlockSpec((ATTN_WIDTH, D_MODEL), const2),
        ],
        out_specs=pl.BlockSpec((bm, D_MODEL), row),
        out_shape=jax.ShapeDtypeStruct((t, D_MODEL), BF16),
        scratch_shapes=[pltpu.VMEM((bm, GMLP_WIDTH), BF16)],
        compiler_params=_params("parallel"),
        name="branch",
    )(gu, vn, attn, gates, gates, w_s, b_s_t, w_a, w_b)


def _post_norm(g_ref, b_ref, o_ref, obf_ref):
    def epilogue(z_ref, rows):
        y = _layer_norm(z_ref[rows, :], g_ref[...], b_ref[...])
        o_ref[rows, :] = y
        obf_ref[rows, :] = y.astype(BF16)
    return epilogue


def _mem_kv_kernel(m_ref, w_ref, o_ref):
    o_ref[...] = _dot(m_ref[...].astype(BF16), w_ref[...].astype(BF16)).astype(BF16)


def _mem_kv(mem2d, w_xkv, layer):
    rows = mem2d.shape[0]
    return pl.pallas_call(
        _mem_kv_kernel,
        grid=(2,),
        in_specs=[
            pl.BlockSpec((rows, D_MODEL), lambda j: (0, 0)),
            pl.BlockSpec((None, D_MODEL, X_WIDTH), lambda j: (layer, 0, j)),
        ],
        out_specs=pl.BlockSpec((rows, X_WIDTH), lambda j: (0, j)),
        out_shape=jax.ShapeDtypeStruct((rows, 2 * X_WIDTH), BF16),
        compiler_params=_params("parallel"),
        name="mem_kv",
    )(mem2d, w_xkv)


def _post_mixer_kernel(n_tiles, m_ref, x_ref, k_ref, v_ref, wo_ref, g1_ref, b1_ref, wq_ref, wxo_ref, g2_ref, b2_ref,
                       o_ref, obf_ref, z0_ref, z1_ref):
    def compute_z():
        z1 = ALPHA * x_ref[...] + _dot(m_ref[...], wo_ref[...])
        x1 = _layer_norm(z1, g1_ref[...], b1_ref[...])
        q = _dot(x1.astype(BF16), wq_ref[...]).astype(BF16)
        scale = X_HEAD_DIM ** -0.5
        outs = []
        for h in range(X_HEADS):
            cols = slice(h * X_HEAD_DIM, (h + 1) * X_HEAD_DIM)
            s = _dot_nt(q[:, cols], k_ref[:, cols]) * scale
            m = jnp.max(s, axis=-1, keepdims=True)
            p = jnp.exp(s - m)
            denom = jnp.sum(p, axis=-1, keepdims=True)
            outs.append(_dot(p.astype(BF16), v_ref[:, cols]) / denom)
        o = jnp.concatenate(outs, axis=1).astype(BF16)
        return ALPHA * x1 + _dot(o, wxo_ref[...])

    _deferred_epilogue(pl.program_id(0), n_tiles, (z0_ref, z1_ref), compute_z,
                       _post_norm(g2_ref, b2_ref, o_ref, obf_ref))


def _post_mixer(merged, x, kv, w_o, ln1_g, ln1_b, w_xq, w_xo, ln2_g, ln2_b, bm, seq):
    t = x.shape[0]
    n_tiles = t // bm
    tiles_per_seq = seq // bm
    batch = lambda s: jnp.minimum(s, n_tiles - 1) // tiles_per_seq
    const2 = lambda s: (0, 0)
    return pl.pallas_call(
        functools.partial(_post_mixer_kernel, n_tiles),
        grid=(n_tiles + 1,),
        in_specs=[
            pl.BlockSpec((bm, D_MODEL), _cur_tile(n_tiles)),
            pl.BlockSpec((bm, D_MODEL), _cur_tile(n_tiles)),
            pl.BlockSpec((MEM_LEN, X_WIDTH), lambda s: (batch(s), 0)),
            pl.BlockSpec((MEM_LEN, X_WIDTH), lambda s: (batch(s), 1)),
            pl.BlockSpec((D_MODEL, D_MODEL), const2),
            pl.BlockSpec((1, D_MODEL), const2),
            pl.BlockSpec((1, D_MODEL), const2),
            pl.BlockSpec((D_MODEL, X_WIDTH), const2),
            pl.BlockSpec((X_WIDTH, D_MODEL), const2),
            pl.BlockSpec((1, D_MODEL), const2),
            pl.BlockSpec((1, D_MODEL), const2),
        ],
        out_specs=[pl.BlockSpec((bm, D_MODEL), _done_tile), pl.BlockSpec((bm, D_MODEL), _done_tile)],
        out_shape=[jax.ShapeDtypeStruct((t, D_MODEL), F32), jax.ShapeDtypeStruct((t, D_MODEL), BF16)],
        scratch_shapes=_acc_scratch(bm, D_MODEL),
        compiler_params=_params("arbitrary"),
        name="post_mixer",
    )(merged, x, kv, kv, w_o, ln1_g, ln1_b, w_xq, w_xo, ln2_g, ln2_b)


def _mlp_kernel(xbf_ref, x_ref, wu_ref, wd_ref, g_ref, b_ref, o_ref, obf_ref, acc_ref):
    f = pl.program_id(1)

    @pl.when(f == 0)
    def _():
        acc_ref[...] = jnp.zeros_like(acc_ref)

    h = jnp.maximum(_dot(xbf_ref[...], wu_ref[...]), 0.0)
    acc_ref[...] += _dot((h * h).astype(BF16), wd_ref[...])

    @pl.when(f == pl.num_programs(1) - 1)
    def _():
        z = ALPHA * x_ref[...] + acc_ref[...]
        y = _layer_norm(z, g_ref[...], b_ref[...])
        o_ref[...] = y
        obf_ref[...] = y.astype(BF16)


def _mlp(x_bf, x, w_up, w_down, ln_g, ln_b, bm, bf):
    t = x.shape[0]
    n_f = D_FF // bf
    row = lambda i, f: (i, 0)
    const2 = lambda i, f: (0, 0)
    late_row = lambda i, f: (jnp.where(f >= n_f // 2, i, jnp.maximum(i - 1, 0)), 0)
    return pl.pallas_call(
        _mlp_kernel,
        grid=(t // bm, n_f),
        in_specs=[
            pl.BlockSpec((bm, D_MODEL), row),
            pl.BlockSpec((bm, D_MODEL), late_row),
            pl.BlockSpec((D_MODEL, bf), lambda i, f: (0, f)),
            pl.BlockSpec((bf, D_MODEL), lambda i, f: (f, 0)),
            pl.BlockSpec((1, D_MODEL), const2),
            pl.BlockSpec((1, D_MODEL), const2),
        ],
        out_specs=[pl.BlockSpec((bm, D_MODEL), row), pl.BlockSpec((bm, D_MODEL), row)],
        out_shape=[jax.ShapeDtypeStruct((t, D_MODEL), F32), jax.ShapeDtypeStruct((t, D_MODEL), BF16)],
        scratch_shapes=[pltpu.VMEM((bm, D_MODEL), F32)],
        compiler_params=_params("parallel", "arbitrary"),
        name="mlp",
    )(x_bf, x, w_up, w_down, ln_g, ln_b)


def _rope_tables(seq):
    inv = 1.0 / (ROPE_THETA ** (jnp.arange(0, HEAD_DIM, 2, dtype=F32) / HEAD_DIM))
    ang = jnp.arange(seq, dtype=F32)[:, None] * inv[None, :]
    cos, sin = jnp.cos(ang), jnp.sin(ang)
    reps = LANES // HEAD_DIM
    cos_t = jnp.tile(jnp.concatenate([cos, cos], axis=1), (1, reps))
    sin_t = jnp.tile(jnp.concatenate([-sin, sin], axis=1), (1, reps))
    return cos_t, sin_t


def kernel(x, mem, w_in, b_gate, ln_v_g, ln_v_b, w_s, b_s, sinks, w_br_a, w_br_b, w_o, ln1_g, ln1_b,
           w_xq, w_xkv, w_xo, ln2_g, ln2_b, w_up, w_down, ln3_g, ln3_b):
    bsz, seq, d = x.shape
    t = bsz * seq
    cos_t, sin_t = _rope_tables(seq)
    xf = x.reshape(t, d)
    mem2d = mem.reshape(bsz * MEM_LEN, d)
    row = lambda a: a.reshape(1, -1)
    for l in range(DEPTH):
        if l == 0:
            gu, xb = _proj_u_from_f32(xf, w_in, l, bm=512)
        else:
            gu = _proj_u(xb, w_in, l, bm=1024)
        vn = _proj_v(xb, w_in, l, row(ln_v_g[l]), row(ln_v_b[l]), bm=1024)
        q = _proj_q(xb, w_in, l, cos_t, sin_t, bm=1024, seq=seq)
        k2, v2 = _proj_kv(xb, w_in, l, cos_t, sin_t, bm=1024, seq=seq)
        gates, w_a, w_b, w_o_bf, w_up_bf, w_down_bf = _proj_gate(
            xb, w_in, l, row(b_gate[l]), [w_br_a, w_br_b, w_o, w_up, w_down], bm=1024, bn=1024)
        attn = _swa(q, k2, v2, sinks[l], bsz, seq)
        merged = _branch(gu, vn, attn, gates, w_s[l], b_s[l].T, w_a, w_b, bm=512)
        kv = _mem_kv(mem2d, w_xkv, l)
        xf, xb = _post_mixer(merged, xf, kv, w_o_bf, row(ln1_g[l]), row(ln1_b[l]), w_xq[l].astype(BF16),
                             w_xo[l].astype(BF16), row(ln2_g[l]), row(ln2_b[l]), bm=512, seq=seq)
        xf, xb = _mlp(xb, xf, w_up_bf, w_down_bf, row(ln3_g[l]), row(ln3_b[l]), bm=512, bf=1024)
    return xf.reshape(bsz, seq, d)
```

```python
import functools

import jax
import jax.numpy as jnp
from jax import lax
from jax.experimental import pallas as pl
from jax.experimental.pallas import tpu as pltpu

D_MODEL = 2048
DEPTH = 2
MEM_LEN = 256
CHUNK = 128
GMLP_WIDTH = 1024
GMLP_GROUPS = 8
N_Q_HEADS = 16
N_KV_HEADS = 4
HEAD_DIM = 64
ATTN_WIDTH = N_Q_HEADS * HEAD_DIM
KV_WIDTH = N_KV_HEADS * HEAD_DIM
WINDOW = 128
ROPE_THETA = 10000.0
X_HEADS = 4
X_HEAD_DIM = 128
X_WIDTH = X_HEADS * X_HEAD_DIM
D_FF = 4 * D_MODEL
LN_EPS = 1e-5
ALPHA = (2 * DEPTH) ** 0.25

OFF_V = GMLP_WIDTH
OFF_Q = 2 * GMLP_WIDTH
OFF_K = OFF_Q + ATTN_WIDTH
OFF_GATE = OFF_K + 2 * KV_WIDTH
GATE_WIDTH = 2 * D_MODEL

LANES = 128
VMEM_LIMIT_BYTES = 56 * 1024 * 1024
SWA_BLOCKS = 4
EPILOGUE_ROWS = 32
SCORE_SCALE = HEAD_DIM ** -0.5

BF16 = jnp.bfloat16
F32 = jnp.float32


def _params(*semantics):
    return pltpu.CompilerParams(dimension_semantics=semantics, vmem_limit_bytes=VMEM_LIMIT_BYTES)


def _dot(a, b):
    return jnp.dot(a, b, preferred_element_type=F32)


def _dot_nt(a, b):
    return lax.dot_general(a, b, (((1,), (1,)), ((), ())), preferred_element_type=F32)


def _layer_norm(z, g, b):
    mu = jnp.mean(z, axis=-1, keepdims=True)
    zc = z - mu
    var = jnp.mean(zc * zc, axis=-1, keepdims=True)
    return zc * lax.rsqrt(var + LN_EPS) * g + b


def _gelu(x):
    return jax.nn.gelu(x, approximate=True)


def _cast_weight_once(w_ref, wbf_ref, step):
    @pl.when(step == 0)
    def _():
        wbf_ref[...] = w_ref[...].astype(BF16)


def _deferred_epilogue(step, n_tiles, acc_refs, compute, epilogue):
    def run_epilogue(acc_ref):
        for r in range(0, acc_ref.shape[0], EPILOGUE_ROWS):
            epilogue(acc_ref, slice(r, r + EPILOGUE_ROWS))

    @pl.when(step == 0)
    def _():
        acc_refs[0][...] = compute()

    for parity in range(2):
        @pl.when((step > 0) & (step < n_tiles) & (step % 2 == parity))
        def _():
            run_epilogue(acc_refs[1 - parity])
            acc_refs[parity][...] = compute()

    @pl.when(step == n_tiles)
    def _():
        run_epilogue(acc_refs[1 - n_tiles % 2])


def _cur_tile(n_tiles):
    return lambda s: (jnp.minimum(s, n_tiles - 1), 0)


def _done_tile(s):
    return (jnp.maximum(s - 1, 0), 0)


def _acc_scratch(bm, width):
    return [pltpu.VMEM((bm, width), F32), pltpu.VMEM((bm, width), F32)]


def _w_in_spec(layer, width, col_block):
    return pl.BlockSpec((None, D_MODEL, width), lambda s: (layer, 0, col_block))


def _proj_u_kernel(x_ref, w_ref, gu_ref, wbf_ref):
    _cast_weight_once(w_ref, wbf_ref, pl.program_id(0))
    gu_ref[...] = _gelu(_dot(x_ref[...], wbf_ref[...]))


def _proj_v_kernel(x_ref, w_ref, g_ref, b_ref, vn_ref, wbf_ref):
    _cast_weight_once(w_ref, wbf_ref, pl.program_id(0))
    v = _gelu(_dot(x_ref[...], wbf_ref[...]))
    vn_ref[...] = _layer_norm(v, g_ref[...], b_ref[...]).astype(BF16)


def _proj_u_from_f32_kernel(x_ref, w_ref, gu_ref, xb_ref, wbf_ref):
    _cast_weight_once(w_ref, wbf_ref, pl.program_id(0))
    xb = x_ref[...].astype(BF16)
    xb_ref[...] = xb
    gu_ref[...] = _gelu(_dot(xb, wbf_ref[...]))


def _proj_u(x_bf, w_in, layer, bm):
    t = x_bf.shape[0]
    return pl.pallas_call(
        _proj_u_kernel,
        grid=(t // bm,),
        in_specs=[
            pl.BlockSpec((bm, D_MODEL), lambda i: (i, 0)),
            _w_in_spec(layer, GMLP_WIDTH, 0),
        ],
        out_specs=pl.BlockSpec((bm, GMLP_WIDTH), lambda i: (i, 0)),
        out_shape=jax.ShapeDtypeStruct((t, GMLP_WIDTH), F32),
        scratch_shapes=[pltpu.VMEM((D_MODEL, GMLP_WIDTH), BF16)],
        compiler_params=_params("arbitrary"),
        name="proj_u",
    )(x_bf, w_in)


def _proj_u_from_f32(x, w_in, layer, bm):
    t = x.shape[0]
    return pl.pallas_call(
        _proj_u_from_f32_kernel,
        grid=(t // bm,),
        in_specs=[
            pl.BlockSpec((bm, D_MODEL), lambda i: (i, 0)),
            _w_in_spec(layer, GMLP_WIDTH, 0),
        ],
        out_specs=[pl.BlockSpec((bm, GMLP_WIDTH), lambda i: (i, 0)), pl.BlockSpec((bm, D_MODEL), lambda i: (i, 0))],
        out_shape=[jax.ShapeDtypeStruct((t, GMLP_WIDTH), F32), jax.ShapeDtypeStruct((t, D_MODEL), BF16)],
        scratch_shapes=[pltpu.VMEM((D_MODEL, GMLP_WIDTH), BF16)],
        compiler_params=_params("arbitrary"),
        name="proj_u_f32",
    )(x, w_in)


def _proj_v(x_bf, w_in, layer, ln_g, ln_b, bm):
    t = x_bf.shape[0]
    return pl.pallas_call(
        _proj_v_kernel,
        grid=(t // bm,),
        in_specs=[
            pl.BlockSpec((bm, D_MODEL), lambda i: (i, 0)),
            _w_in_spec(layer, GMLP_WIDTH, OFF_V // GMLP_WIDTH),
            pl.BlockSpec((1, GMLP_WIDTH), lambda i: (0, 0)),
            pl.BlockSpec((1, GMLP_WIDTH), lambda i: (0, 0)),
        ],
        out_specs=pl.BlockSpec((bm, GMLP_WIDTH), lambda i: (i, 0)),
        out_shape=jax.ShapeDtypeStruct((t, GMLP_WIDTH), BF16),
        scratch_shapes=[pltpu.VMEM((D_MODEL, GMLP_WIDTH), BF16)],
        compiler_params=_params("arbitrary"),
        name="proj_v",
    )(x_bf, w_in, ln_g, ln_b)


def _rope(x, cos, sin_signed, first_half):
    outs = []
    for c in range(x.shape[1] // LANES):
        xs = x[:, c * LANES:(c + 1) * LANES]
        swapped = jnp.where(first_half, pltpu.roll(xs, LANES - HEAD_DIM // 2, 1),
                            pltpu.roll(xs, HEAD_DIM // 2, 1))
        outs.append(xs * cos + swapped * sin_signed)
    return outs


def _duplicate_heads(cols, low_half):
    outs = []
    for xs in cols:
        other = pltpu.roll(xs, HEAD_DIM, 1)
        outs.append(jnp.where(low_half, xs, other))
        outs.append(jnp.where(low_half, other, xs))
    return outs


def _head_masks(shape):
    lane = lax.broadcasted_iota(jnp.int32, shape, 1)
    return (lane % HEAD_DIM) < (HEAD_DIM // 2), (lane % LANES) < HEAD_DIM


def _proj_q_kernel(n_tiles, x_ref, w_ref, cos_ref, sin_ref, q_ref, wbf_ref, acc0_ref, acc1_ref):
    step = pl.program_id(0)
    _cast_weight_once(w_ref, wbf_ref, step)

    def epilogue(acc_ref, rows):
        first_half, low_half = _head_masks((EPILOGUE_ROWS, LANES))
        cols = _rope(acc_ref[rows, :], cos_ref[rows, :] * SCORE_SCALE, sin_ref[rows, :] * SCORE_SCALE, first_half)
        padded = []
        for c in cols:
            padded.append(jnp.where(low_half, c, 0.0))
            padded.append(jnp.where(low_half, 0.0, c))
        q_ref[rows, :] = jnp.concatenate(padded, axis=1).astype(BF16)

    _deferred_epilogue(step, n_tiles, (acc0_ref, acc1_ref), lambda: _dot(x_ref[...], wbf_ref[...]), epilogue)


def _proj_kv_kernel(n_tiles, x_ref, w_ref, cos_ref, sin_ref, k_ref, v_ref, wbf_ref, acc0_ref, acc1_ref):
    step = pl.program_id(0)
    _cast_weight_once(w_ref, wbf_ref, step)

    def epilogue(acc_ref, rows):
        first_half, low_half = _head_masks((EPILOGUE_ROWS, LANES))
        acc = acc_ref[rows, :]
        k_cols = _rope(acc[:, :KV_WIDTH], cos_ref[rows, :], sin_ref[rows, :], first_half)
        v_cols = [acc[:, KV_WIDTH + c * LANES:KV_WIDTH + (c + 1) * LANES] for c in range(KV_WIDTH // LANES)]
        k_ref[rows, :] = jnp.concatenate(_duplicate_heads(k_cols, low_half), axis=1).astype(BF16)
        v_ref[rows, :] = jnp.concatenate(_duplicate_heads(v_cols, low_half), axis=1).astype(BF16)

    _deferred_epilogue(step, n_tiles, (acc0_ref, acc1_ref), lambda: _dot(x_ref[...], wbf_ref[...]), epilogue)


def _rope_table_spec(bm, tiles_per_seq):
    return pl.BlockSpec((bm, LANES), lambda s: (jnp.maximum(s - 1, 0) % tiles_per_seq, 0))


def _proj_q(x_bf, w_in, layer, cos_t, sin_t, bm, seq):
    t = x_bf.shape[0]
    n_tiles = t // bm
    return pl.pallas_call(
        functools.partial(_proj_q_kernel, n_tiles),
        grid=(n_tiles + 1,),
        in_specs=[
            pl.BlockSpec((bm, D_MODEL), _cur_tile(n_tiles)),
            _w_in_spec(layer, ATTN_WIDTH, OFF_Q // ATTN_WIDTH),
            _rope_table_spec(bm, seq // bm),
            _rope_table_spec(bm, seq // bm),
        ],
        out_specs=pl.BlockSpec((bm, 2 * ATTN_WIDTH), _done_tile),
        out_shape=jax.ShapeDtypeStruct((t, 2 * ATTN_WIDTH), BF16),
        scratch_shapes=[pltpu.VMEM((D_MODEL, ATTN_WIDTH), BF16)] + _acc_scratch(bm, ATTN_WIDTH),
        compiler_params=_params("arbitrary"),
        name="proj_q",
    )(x_bf, w_in, cos_t, sin_t)


def _proj_kv(x_bf, w_in, layer, cos_t, sin_t, bm, seq):
    t = x_bf.shape[0]
    n_tiles = t // bm
    return pl.pallas_call(
        functools.partial(_proj_kv_kernel, n_tiles),
        grid=(n_tiles + 1,),
        in_specs=[
            pl.BlockSpec((bm, D_MODEL), _cur_tile(n_tiles)),
            _w_in_spec(layer, 2 * KV_WIDTH, OFF_K // (2 * KV_WIDTH)),
            _rope_table_spec(bm, seq // bm),
            _rope_table_spec(bm, seq // bm),
        ],
        out_specs=[
            pl.BlockSpec((bm, 2 * KV_WIDTH), _done_tile),
            pl.BlockSpec((bm, 2 * KV_WIDTH), _done_tile),
        ],
        out_shape=[
            jax.ShapeDtypeStruct((t, 2 * KV_WIDTH), BF16),
            jax.ShapeDtypeStruct((t, 2 * KV_WIDTH), BF16),
        ],
        scratch_shapes=[pltpu.VMEM((D_MODEL, 2 * KV_WIDTH), BF16)] + _acc_scratch(bm, 2 * KV_WIDTH),
        compiler_params=_params("arbitrary"),
        name="proj_kv",
    )(x_bf, w_in, cos_t, sin_t)


def _proj_gate_kernel(n_tiles, m_tiles, n_side, x_ref, w0_ref, w1_ref, b_ref, *refs):
    side_in, o_ref, side_out = refs[:n_side], refs[n_side], refs[n_side + 1:2 * n_side + 1]
    wbf_ref, acc0_ref, acc1_ref = refs[2 * n_side + 1:]
    step = pl.program_id(0)
    half = w0_ref.shape[1]

    @pl.when((step % m_tiles == 0) & (step < n_tiles))
    def _():
        wbf_ref[:, :half] = w0_ref[...].astype(BF16)
        wbf_ref[:, half:] = w1_ref[...].astype(BF16)

    def compute():
        for src, dst in zip(side_in, side_out):
            dst[...] = src[...].astype(BF16)
        return _dot(x_ref[...], wbf_ref[...])

    def epilogue(acc_ref, rows):
        o_ref[rows, :] = jax.nn.sigmoid(acc_ref[rows, :] + b_ref[...])

    _deferred_epilogue(step, n_tiles, (acc0_ref, acc1_ref), compute, epilogue)


def _proj_gate(x_bf, w_in, layer, b_gate, side_weights, bm, bn):
    t = x_bf.shape[0]
    m_tiles = t // bm
    n_tiles = m_tiles * (GATE_WIDTH // bn)
    half = bn // 2
    first = OFF_GATE // half
    cur = lambda s: jnp.minimum(s, n_tiles - 1)
    done = lambda s: jnp.maximum(s - 1, 0)
    slab = lambda w: (w.shape[1] // n_tiles, w.shape[2])
    return pl.pallas_call(
        functools.partial(_proj_gate_kernel, n_tiles, m_tiles, len(side_weights)),
        grid=(n_tiles + 1,),
        in_specs=[
            pl.BlockSpec((bm, D_MODEL), lambda s: (cur(s) % m_tiles, 0)),
            pl.BlockSpec((None, D_MODEL, half), lambda s: (layer, 0, first + 2 * (cur(s) // m_tiles))),
            pl.BlockSpec((None, D_MODEL, half), lambda s: (layer, 0, first + 2 * (cur(s) // m_tiles) + 1)),
            pl.BlockSpec((1, bn), lambda s: (0, done(s) // m_tiles)),
        ] + [pl.BlockSpec((None,) + slab(w), lambda s: (layer, cur(s), 0)) for w in side_weights],
        out_specs=[pl.BlockSpec((bm, bn), lambda s: (done(s) % m_tiles, done(s) // m_tiles))]
        + [pl.BlockSpec(slab(w), lambda s: (cur(s), 0)) for w in side_weights],
        out_shape=[jax.ShapeDtypeStruct((t, GATE_WIDTH), F32)]
        + [jax.ShapeDtypeStruct(w.shape[1:], BF16) for w in side_weights],
        scratch_shapes=[pltpu.VMEM((D_MODEL, bn), BF16)] + _acc_scratch(bm, bn),
        compiler_params=_params("arbitrary"),
        name="proj_gate",
    )(x_bf, w_in, w_in, b_gate, *side_weights)


def _swa_kernel(sink_ref, q_ref, kp_ref, kc_ref, vp_ref, vc_ref, o_ref):
    n = pl.program_id(1)
    pair_rows = 2 * WINDOW
    lane = lax.broadcasted_iota(jnp.int32, (WINDOW, LANES), 1)
    low_half = lane < HEAD_DIM
    row = lax.broadcasted_iota(jnp.int32, (pair_rows, WINDOW), 0)
    key = lax.broadcasted_iota(jnp.int32, (pair_rows, WINDOW), 1)
    in_own_block = key <= (row % WINDOW)
    top_rows = lax.broadcasted_iota(jnp.int32, (pair_rows, 1), 0) < WINDOW
    first_key = jnp.where(n > 0, 0, WINDOW)
    first_bias = jnp.where(lax.broadcasted_iota(jnp.int32, (1, WINDOW), 1) >= first_key, 0.0, -jnp.inf)
    for blk in range(SWA_BLOCKS):
        own = slice(blk * WINDOW, (blk + 1) * WINDOW)
        outs = []
        for p in range(N_Q_HEADS // 2):
            g = (2 * p) // (N_Q_HEADS // N_KV_HEADS)
            kv_cols = slice(g * LANES, (g + 1) * LANES)
            if blk == 0:
                k_prev, v_prev = kp_ref[:, kv_cols], vp_ref[:, kv_cols]
            else:
                before = slice((blk - 1) * WINDOW, blk * WINDOW)
                k_prev, v_prev = kc_ref[before, kv_cols], vc_ref[before, kv_cols]
            qs = jnp.concatenate([q_ref[own, 2 * p * LANES:(2 * p + 1) * LANES],
                                  q_ref[own, (2 * p + 1) * LANES:(2 * p + 2) * LANES]], axis=0)
            s_prev = _dot_nt(qs, k_prev)
            if blk == 0:
                s_prev = s_prev + first_bias
            s = jnp.where(in_own_block, _dot_nt(qs, kc_ref[own, kv_cols]), s_prev)
            sink = jnp.where(top_rows, sink_ref[2 * p], sink_ref[2 * p + 1])
            m = jnp.maximum(jnp.max(s, axis=-1, keepdims=True), sink)
            e = jnp.exp(s - m)
            denom = jnp.sum(e, axis=-1, keepdims=True) + jnp.exp(sink - m)
            e_own = jnp.where(in_own_block, e, 0.0).astype(BF16)
            e_prev = jnp.where(in_own_block, 0.0, e).astype(BF16)
            o2 = (_dot(e_own, vc_ref[own, kv_cols]) + _dot(e_prev, v_prev)) / denom
            outs.append(jnp.where(low_half, o2[:WINDOW], o2[WINDOW:]))
        o_ref[own, :] = jnp.concatenate(outs, axis=1).astype(BF16)


def _swa(q, k2, v2, sinks, bsz, seq):
    steps_per_seq = seq // (SWA_BLOCKS * WINDOW)
    cur = lambda b, n: (b * steps_per_seq + n, 0)
    prev = lambda b, n: (SWA_BLOCKS * (b * steps_per_seq + n) - jnp.minimum(n, 1), 0)
    return pl.pallas_call(
        _swa_kernel,
        grid=(bsz, steps_per_seq),
        in_specs=[
            pl.BlockSpec(memory_space=pltpu.SMEM),
            pl.BlockSpec((SWA_BLOCKS * WINDOW, 2 * ATTN_WIDTH), cur),
            pl.BlockSpec((WINDOW, 2 * KV_WIDTH), prev),
            pl.BlockSpec((SWA_BLOCKS * WINDOW, 2 * KV_WIDTH), cur),
            pl.BlockSpec((WINDOW, 2 * KV_WIDTH), prev),
            pl.BlockSpec((SWA_BLOCKS * WINDOW, 2 * KV_WIDTH), cur),
        ],
        out_specs=pl.BlockSpec((SWA_BLOCKS * WINDOW, ATTN_WIDTH), cur),
        out_shape=jax.ShapeDtypeStruct((bsz * seq, ATTN_WIDTH), BF16),
        compiler_params=_params("parallel", "parallel"),
        name="swa",
    )(sinks, q, k2, k2, v2, v2)


def _branch_kernel(gu_ref, vn_ref, at_ref, ga_ref, gb_ref, ws_ref, bs_ref, wa_ref, wb_ref, o_ref, gated_ref):
    bm = gu_ref.shape[0]
    t_idx = lax.broadcasted_iota(jnp.int32, (CHUNK, CHUNK), 0)
    s_idx = lax.broadcasted_iota(jnp.int32, (CHUNK, CHUNK), 1)
    causal = s_idx <= t_idx
    bs = bs_ref[...]
    for g in range(GMLP_GROUPS):
        w = jnp.where(causal, ws_ref[g], 0.0).astype(BF16)
        bias = bs[:, g:g + 1]
        cols = slice(g * LANES, (g + 1) * LANES)
        for c in range(bm // CHUNK):
            rows = slice(c * CHUNK, (c + 1) * CHUNK)
            mixed = _dot(w, vn_ref[rows, cols]) + bias
            gated_ref[rows, cols] = (gu_ref[rows, cols] * mixed).astype(BF16)
    ya = _dot(gated_ref[...], wa_ref[...])
    yb = _dot(at_ref[...], wb_ref[...])
    o_ref[...] = (ga_ref[...] * ya + gb_ref[...] * yb).astype(BF16)


def _branch(gu, vn, attn, gates, w_s, b_s_t, w_a, w_b, bm):
    t = gu.shape[0]
    row = lambda i: (i, 0)
    const2 = lambda i: (0, 0)
    return pl.pallas_call(
        _branch_kernel,
        grid=(t // bm,),
        in_specs=[
            pl.BlockSpec((bm, GMLP_WIDTH), row),
            pl.BlockSpec((bm, GMLP_WIDTH), row),
            pl.BlockSpec((bm, ATTN_WIDTH), row),
            pl.BlockSpec((bm, D_MODEL), lambda i: (i, 0)),
            pl.BlockSpec((bm, D_MODEL), lambda i: (i, 1)),
            pl.BlockSpec((GMLP_GROUPS, CHUNK, CHUNK), lambda i: (0, 0, 0)),
            pl.BlockSpec((CHUNK, GMLP_GROUPS), const2),
            pl.BlockSpec((GMLP_WIDTH, D_MODEL), const2),
            pl.BlockSpec((ATTN_WIDTH, D_MODEL), const2),
        ],
        out_specs=pl.BlockSpec((bm, D_MODEL), row),
        out_shape=jax.ShapeDtypeStruct((t, D_MODEL), BF16),
        scratch_shapes=[pltpu.VMEM((bm, GMLP_WIDTH), BF16)],
        compiler_params=_params("parallel"),
        name="branch",
    )(gu, vn, attn, gates, gates, w_s, b_s_t, w_a, w_b)


def _post_norm(g_ref, b_ref, o_ref, obf_ref):
    def epilogue(z_ref, rows):
        y = _layer_norm(z_ref[rows, :], g_ref[...], b_ref[...])
        o_ref[rows, :] = y
        obf_ref[rows, :] = y.astype(BF16)
    return epilogue


def _mem_kv_kernel(m_ref, w_ref, o_ref):
    o_ref[...] = _dot(m_ref[...].astype(BF16), w_ref[...].astype(BF16)).astype(BF16)


def _mem_kv(mem2d, w_xkv, layer):
    rows = mem2d.shape[0]
    return pl.pallas_call(
        _mem_kv_kernel,
        grid=(2,),
        in_specs=[
            pl.BlockSpec((rows, D_MODEL), lambda j: (0, 0)),
            pl.BlockSpec((None, D_MODEL, X_WIDTH), lambda j: (layer, 0, j)),
        ],
        out_specs=pl.BlockSpec((rows, X_WIDTH), lambda j: (0, j)),
        out_shape=jax.ShapeDtypeStruct((rows, 2 * X_WIDTH), BF16),
        compiler_params=_params("parallel"),
        name="mem_kv",
    )(mem2d, w_xkv)


def _post_mixer_kernel(n_tiles, m_ref, x_ref, k_ref, v_ref, wo_ref, g1_ref, b1_ref, wq_ref, wxo_ref, g2_ref, b2_ref,
                       o_ref, obf_ref, z0_ref, z1_ref):
    def compute_z():
        z1 = ALPHA * x_ref[...] + _dot(m_ref[...], wo_ref[...])
        x1 = _layer_norm(z1, g1_ref[...], b1_ref[...])
        q = _dot(x1.astype(BF16), wq_ref[...]).astype(BF16)
        scale = X_HEAD_DIM ** -0.5
        outs = []
        for h in range(X_HEADS):
            cols = slice(h * X_HEAD_DIM, (h + 1) * X_HEAD_DIM)
            s = _dot_nt(q[:, cols], k_ref[:, cols]) * scale
            m = jnp.max(s, axis=-1, keepdims=True)
            p = jnp.exp(s - m)
            denom = jnp.sum(p, axis=-1, keepdims=True)
            outs.append(_dot(p.astype(BF16), v_ref[:, cols]) / denom)
        o = jnp.concatenate(outs, axis=1).astype(BF16)
        return ALPHA * x1 + _dot(o, wxo_ref[...])

    _deferred_epilogue(pl.program_id(0), n_tiles, (z0_ref, z1_ref), compute_z,
                       _post_norm(g2_ref, b2_ref, o_ref, obf_ref))


def _post_mixer(merged, x, kv, w_o, ln1_g, ln1_b, w_xq, w_xo, ln2_g, ln2_b, bm, seq):
    t = x.shape[0]
    n_tiles = t // bm
    tiles_per_seq = seq // bm
    batch = lambda s: jnp.minimum(s, n_tiles - 1) // tiles_per_seq
    const2 = lambda s: (0, 0)
    return pl.pallas_call(
        functools.partial(_post_mixer_kernel, n_tiles),
        grid=(n_tiles + 1,),
        in_specs=[
            pl.BlockSpec((bm, D_MODEL), _cur_tile(n_tiles)),
            pl.BlockSpec((bm, D_MODEL), _cur_tile(n_tiles)),
            pl.BlockSpec((MEM_LEN, X_WIDTH), lambda s: (batch(s), 0)),
            pl.BlockSpec((MEM_LEN, X_WIDTH), lambda s: (batch(s), 1)),
            pl.BlockSpec((D_MODEL, D_MODEL), const2),
            pl.BlockSpec((1, D_MODEL), const2),
            pl.BlockSpec((1, D_MODEL), const2),
            pl.BlockSpec((D_MODEL, X_WIDTH), const2),
            pl.BlockSpec((X_WIDTH, D_MODEL), const2),
            pl.BlockSpec((1, D_MODEL), const2),
            pl.BlockSpec((1, D_MODEL), const2),
        ],
        out_specs=[pl.BlockSpec((bm, D_MODEL), _done_tile), pl.BlockSpec((bm, D_MODEL), _done_tile)],
        out_shape=[jax.ShapeDtypeStruct((t, D_MODEL), F32), jax.ShapeDtypeStruct((t, D_MODEL), BF16)],
        scratch_shapes=_acc_scratch(bm, D_MODEL),
        compiler_params=_params("arbitrary"),
        name="post_mixer",
    )(merged, x, kv, kv, w_o, ln1_g, ln1_b, w_xq, w_xo, ln2_g, ln2_b)


def _mlp_kernel(xbf_ref, x_ref, wu_ref, wd_ref, g_ref, b_ref, o_ref, obf_ref, acc_ref):
    f = pl.program_id(1)

    def contribution():
        h = jnp.maximum(_dot(xbf_ref[...], wu_ref[...]), 0.0)
        return _dot((h * h).astype(BF16), wd_ref[...])

    @pl.when(f == 0)
    def _():
        acc_ref[...] = ALPHA * x_ref[...] + contribution()

    @pl.when(f > 0)
    def _():
        acc_ref[...] += contribution()

    @pl.when(f == pl.num_programs(1) - 1)
    def _():
        y = _layer_norm(acc_ref[...], g_ref[...], b_ref[...])
        o_ref[...] = y
        obf_ref[...] = y.astype(BF16)


def _mlp(x_bf, x, w_up, w_down, ln_g, ln_b, bm, bf):
    t = x.shape[0]
    n_f = D_FF // bf
    row = lambda i, f: (i, 0)
    const2 = lambda i, f: (0, 0)
    return pl.pallas_call(
        _mlp_kernel,
        grid=(t // bm, n_f),
        in_specs=[
            pl.BlockSpec((bm, D_MODEL), row),
            pl.BlockSpec((bm, D_MODEL), row),
            pl.BlockSpec((D_MODEL, bf), lambda i, f: (0, f)),
            pl.BlockSpec((bf, D_MODEL), lambda i, f: (f, 0)),
            pl.BlockSpec((1, D_MODEL), const2),
            pl.BlockSpec((1, D_MODEL), const2),
        ],
        out_specs=[pl.BlockSpec((bm, D_MODEL), row), pl.BlockSpec((bm, D_MODEL), row)],
        out_shape=[jax.ShapeDtypeStruct((t, D_MODEL), F32), jax.ShapeDtypeStruct((t, D_MODEL), BF16)],
        scratch_shapes=[pltpu.VMEM((bm, D_MODEL), F32)],
        compiler_params=_params("parallel", "arbitrary"),
        name="mlp",
    )(x_bf, x, w_up, w_down, ln_g, ln_b)


def _rope_tables(seq):
    inv = 1.0 / (ROPE_THETA ** (jnp.arange(0, HEAD_DIM, 2, dtype=F32) / HEAD_DIM))
    ang = jnp.arange(seq, dtype=F32)[:, None] * inv[None, :]
    cos, sin = jnp.cos(ang), jnp.sin(ang)
    reps = LANES // HEAD_DIM
    cos_t = jnp.tile(jnp.concatenate([cos, cos], axis=1), (1, reps))
    sin_t = jnp.tile(jnp.concatenate([-sin, sin], axis=1), (1, reps))
    return cos_t, sin_t


def kernel(x, mem, w_in, b_gate, ln_v_g, ln_v_b, w_s, b_s, sinks, w_br_a, w_br_b, w_o, ln1_g, ln1_b,
           w_xq, w_xkv, w_xo, ln2_g, ln2_b, w_up, w_down, ln3_g, ln3_b):
    bsz, seq, d = x.shape
    t = bsz * seq
    cos_t, sin_t = _rope_tables(seq)
    xf = x.reshape(t, d)
    mem2d = mem.reshape(bsz * MEM_LEN, d)
    row = lambda a: a.reshape(1, -1)
    for l in range(DEPTH):
        if l == 0:
            gu, xb = _proj_u_from_f32(xf, w_in, l, bm=512)
        else:
            gu = _proj_u(xb, w_in, l, bm=1024)
        vn = _proj_v(xb, w_in, l, row(ln_v_g[l]), row(ln_v_b[l]), bm=1024)
        q = _proj_q(xb, w_in, l, cos_t, sin_t, bm=1024, seq=seq)
        k2, v2 = _proj_kv(xb, w_in, l, cos_t, sin_t, bm=1024, seq=seq)
        gates, w_a, w_b, w_o_bf, w_up_bf, w_down_bf = _proj_gate(
            xb, w_in, l, row(b_gate[l]), [w_br_a, w_br_b, w_o, w_up, w_down], bm=1024, bn=1024)
        attn = _swa(q, k2, v2, sinks[l], bsz, seq)
        merged = _branch(gu, vn, attn, gates, w_s[l], b_s[l].T, w_a, w_b, bm=512)
        kv = _mem_kv(mem2d, w_xkv, l)
        xf, xb = _post_mixer(merged, xf, kv, w_o_bf, row(ln1_g[l]), row(ln1_b[l]), w_xq[l].astype(BF16),
                             w_xo[l].astype(BF16), row(ln2_g[l]), row(ln2_b[l]), bm=512, seq=seq)
        xf, xb = _mlp(xb, xf, w_up_bf, w_down_bf, row(ln3_g[l]), row(ln3_b[l]), bm=512, bf=1024)
    return xf.reshape(bsz, seq, d)
```

```python
import functools

import jax
import jax.numpy as jnp
from jax import lax
from jax.experimental import pallas as pl
from jax.experimental.pallas import tpu as pltpu

D_MODEL = 2048
DEPTH = 2
MEM_LEN = 256
CHUNK = 128
GMLP_WIDTH = 1024
GMLP_GROUPS = 8
N_Q_HEADS = 16
N_KV_HEADS = 4
HEAD_DIM = 64
ATTN_WIDTH = N_Q_HEADS * HEAD_DIM
KV_WIDTH = N_KV_HEADS * HEAD_DIM
WINDOW = 128
ROPE_THETA = 10000.0
X_HEADS = 4
X_HEAD_DIM = 128
X_WIDTH = X_HEADS * X_HEAD_DIM
D_FF = 4 * D_MODEL
LN_EPS = 1e-5
ALPHA = (2 * DEPTH) ** 0.25

OFF_V = GMLP_WIDTH
OFF_Q = 2 * GMLP_WIDTH
OFF_K = OFF_Q + ATTN_WIDTH
OFF_GATE = OFF_K + 2 * KV_WIDTH
GATE_WIDTH = 2 * D_MODEL

LANES = 128
VMEM_LIMIT_BYTES = 56 * 1024 * 1024
SWA_BLOCKS = 8
EPILOGUE_ROWS = 32
SCORE_SCALE = HEAD_DIM ** -0.5

BF16 = jnp.bfloat16
F32 = jnp.float32


def _params(*semantics):
    return pltpu.CompilerParams(dimension_semantics=semantics, vmem_limit_bytes=VMEM_LIMIT_BYTES)


def _dot(a, b):
    return jnp.dot(a, b, preferred_element_type=F32)


def _dot_nt(a, b):
    return lax.dot_general(a, b, (((1,), (1,)), ((), ())), preferred_element_type=F32)


def _layer_norm(z, g, b):
    mu = jnp.mean(z, axis=-1, keepdims=True)
    zc = z - mu
    var = jnp.mean(zc * zc, axis=-1, keepdims=True)
    return zc * lax.rsqrt(var + LN_EPS) * g + b


def _gelu(x):
    return jax.nn.gelu(x, approximate=True)


def _cast_weight_once(w_ref, wbf_ref, step):
    @pl.when(step == 0)
    def _():
        wbf_ref[...] = w_ref[...].astype(BF16)


def _deferred_epilogue(step, n_tiles, acc_refs, compute, epilogue):
    def run_epilogue(acc_ref):
        for r in range(0, acc_ref.shape[0], EPILOGUE_ROWS):
            epilogue(acc_ref, slice(r, r + EPILOGUE_ROWS))

    @pl.when(step == 0)
    def _():
        acc_refs[0][...] = compute()

    for parity in range(2):
        @pl.when((step > 0) & (step < n_tiles) & (step % 2 == parity))
        def _():
            run_epilogue(acc_refs[1 - parity])
            acc_refs[parity][...] = compute()

    @pl.when(step == n_tiles)
    def _():
        run_epilogue(acc_refs[1 - n_tiles % 2])


def _cur_tile(n_tiles):
    return lambda s: (jnp.minimum(s, n_tiles - 1), 0)


def _done_tile(s):
    return (jnp.maximum(s - 1, 0), 0)


def _acc_scratch(bm, width):
    return [pltpu.VMEM((bm, width), F32), pltpu.VMEM((bm, width), F32)]


def _w_in_spec(layer, width, col_block):
    return pl.BlockSpec((None, D_MODEL, width), lambda s: (layer, 0, col_block))


def _proj_u_kernel(x_ref, w_ref, gu_ref, wbf_ref):
    _cast_weight_once(w_ref, wbf_ref, pl.program_id(0))
    gu_ref[...] = _gelu(_dot(x_ref[...], wbf_ref[...]))


def _proj_v_kernel(x_ref, w_ref, g_ref, b_ref, vn_ref, wbf_ref):
    _cast_weight_once(w_ref, wbf_ref, pl.program_id(0))
    v = _gelu(_dot(x_ref[...], wbf_ref[...]))
    vn_ref[...] = _layer_norm(v, g_ref[...], b_ref[...]).astype(BF16)


def _proj_u_from_f32_kernel(x_ref, w_ref, gu_ref, xb_ref, wbf_ref):
    _cast_weight_once(w_ref, wbf_ref, pl.program_id(0))
    xb = x_ref[...].astype(BF16)
    xb_ref[...] = xb
    gu_ref[...] = _gelu(_dot(xb, wbf_ref[...]))


def _proj_u(x_bf, w_in, layer, bm):
    t = x_bf.shape[0]
    return pl.pallas_call(
        _proj_u_kernel,
        grid=(t // bm,),
        in_specs=[
            pl.BlockSpec((bm, D_MODEL), lambda i: (i, 0)),
            _w_in_spec(layer, GMLP_WIDTH, 0),
        ],
        out_specs=pl.BlockSpec((bm, GMLP_WIDTH), lambda i: (i, 0)),
        out_shape=jax.ShapeDtypeStruct((t, GMLP_WIDTH), F32),
        scratch_shapes=[pltpu.VMEM((D_MODEL, GMLP_WIDTH), BF16)],
        compiler_params=_params("arbitrary"),
        name="proj_u",
    )(x_bf, w_in)


def _proj_u_from_f32(x, w_in, layer, bm):
    t = x.shape[0]
    return pl.pallas_call(
        _proj_u_from_f32_kernel,
        grid=(t // bm,),
        in_specs=[
            pl.BlockSpec((bm, D_MODEL), lambda i: (i, 0)),
            _w_in_spec(layer, GMLP_WIDTH, 0),
        ],
        out_specs=[pl.BlockSpec((bm, GMLP_WIDTH), lambda i: (i, 0)), pl.BlockSpec((bm, D_MODEL), lambda i: (i, 0))],
        out_shape=[jax.ShapeDtypeStruct((t, GMLP_WIDTH), F32), jax.ShapeDtypeStruct((t, D_MODEL), BF16)],
        scratch_shapes=[pltpu.VMEM((D_MODEL, GMLP_WIDTH), BF16)],
        compiler_params=_params("arbitrary"),
        name="proj_u_f32",
    )(x, w_in)


def _proj_v(x_bf, w_in, layer, ln_g, ln_b, bm):
    t = x_bf.shape[0]
    return pl.pallas_call(
        _proj_v_kernel,
        grid=(t // bm,),
        in_specs=[
            pl.BlockSpec((bm, D_MODEL), lambda i: (i, 0)),
            _w_in_spec(layer, GMLP_WIDTH, OFF_V // GMLP_WIDTH),
            pl.BlockSpec((1, GMLP_WIDTH), lambda i: (0, 0)),
            pl.BlockSpec((1, GMLP_WIDTH), lambda i: (0, 0)),
        ],
        out_specs=pl.BlockSpec((bm, GMLP_WIDTH), lambda i: (i, 0)),
        out_shape=jax.ShapeDtypeStruct((t, GMLP_WIDTH), BF16),
        scratch_shapes=[pltpu.VMEM((D_MODEL, GMLP_WIDTH), BF16)],
        compiler_params=_params("arbitrary"),
        name="proj_v",
    )(x_bf, w_in, ln_g, ln_b)


def _rope(x, cos, sin_signed, first_half):
    outs = []
    for c in range(x.shape[1] // LANES):
        xs = x[:, c * LANES:(c + 1) * LANES]
        swapped = jnp.where(first_half, pltpu.roll(xs, LANES - HEAD_DIM // 2, 1),
                            pltpu.roll(xs, HEAD_DIM // 2, 1))
        outs.append(xs * cos + swapped * sin_signed)
    return outs


def _duplicate_heads(cols, low_half):
    outs = []
    for xs in cols:
        other = pltpu.roll(xs, HEAD_DIM, 1)
        outs.append(jnp.where(low_half, xs, other))
        outs.append(jnp.where(low_half, other, xs))
    return outs


def _head_masks(shape):
    lane = lax.broadcasted_iota(jnp.int32, shape, 1)
    return (lane % HEAD_DIM) < (HEAD_DIM // 2), (lane % LANES) < HEAD_DIM


def _proj_q_kernel(n_tiles, x_ref, w_ref, cos_ref, sin_ref, q_ref, wbf_ref, acc0_ref, acc1_ref):
    step = pl.program_id(0)
    _cast_weight_once(w_ref, wbf_ref, step)

    def epilogue(acc_ref, rows):
        first_half, low_half = _head_masks((EPILOGUE_ROWS, LANES))
        cols = _rope(acc_ref[rows, :], cos_ref[rows, :] * SCORE_SCALE, sin_ref[rows, :] * SCORE_SCALE, first_half)
        padded = []
        for c in cols:
            padded.append(jnp.where(low_half, c, 0.0))
            padded.append(jnp.where(low_half, 0.0, c))
        q_ref[rows, :] = jnp.concatenate(padded, axis=1).astype(BF16)

    _deferred_epilogue(step, n_tiles, (acc0_ref, acc1_ref), lambda: _dot(x_ref[...], wbf_ref[...]), epilogue)


def _proj_kv_kernel(n_tiles, x_ref, w_ref, cos_ref, sin_ref, k_ref, v_ref, wbf_ref, acc0_ref, acc1_ref):
    step = pl.program_id(0)
    _cast_weight_once(w_ref, wbf_ref, step)

    def epilogue(acc_ref, rows):
        first_half, low_half = _head_masks((EPILOGUE_ROWS, LANES))
        acc = acc_ref[rows, :]
        k_cols = _rope(acc[:, :KV_WIDTH], cos_ref[rows, :], sin_ref[rows, :], first_half)
        v_cols = [acc[:, KV_WIDTH + c * LANES:KV_WIDTH + (c + 1) * LANES] for c in range(KV_WIDTH // LANES)]
        k_ref[rows, :] = jnp.concatenate(_duplicate_heads(k_cols, low_half), axis=1).astype(BF16)
        v_ref[rows, :] = jnp.concatenate(_duplicate_heads(v_cols, low_half), axis=1).astype(BF16)

    _deferred_epilogue(step, n_tiles, (acc0_ref, acc1_ref), lambda: _dot(x_ref[...], wbf_ref[...]), epilogue)


def _rope_table_spec(bm, tiles_per_seq):
    return pl.BlockSpec((bm, LANES), lambda s: (jnp.maximum(s - 1, 0) % tiles_per_seq, 0))


def _proj_q(x_bf, w_in, layer, cos_t, sin_t, bm, seq):
    t = x_bf.shape[0]
    n_tiles = t // bm
    return pl.pallas_call(
        functools.partial(_proj_q_kernel, n_tiles),
        grid=(n_tiles + 1,),
        in_specs=[
            pl.BlockSpec((bm, D_MODEL), _cur_tile(n_tiles)),
            _w_in_spec(layer, ATTN_WIDTH, OFF_Q // ATTN_WIDTH),
            _rope_table_spec(bm, seq // bm),
            _rope_table_spec(bm, seq // bm),
        ],
        out_specs=pl.BlockSpec((bm, 2 * ATTN_WIDTH), _done_tile),
        out_shape=jax.ShapeDtypeStruct((t, 2 * ATTN_WIDTH), BF16),
        scratch_shapes=[pltpu.VMEM((D_MODEL, ATTN_WIDTH), BF16)] + _acc_scratch(bm, ATTN_WIDTH),
        compiler_params=_params("arbitrary"),
        name="proj_q",
    )(x_bf, w_in, cos_t, sin_t)


def _proj_kv(x_bf, w_in, layer, cos_t, sin_t, bm, seq):
    t = x_bf.shape[0]
    n_tiles = t // bm
    return pl.pallas_call(
        functools.partial(_proj_kv_kernel, n_tiles),
        grid=(n_tiles + 1,),
        in_specs=[
            pl.BlockSpec((bm, D_MODEL), _cur_tile(n_tiles)),
            _w_in_spec(layer, 2 * KV_WIDTH, OFF_K // (2 * KV_WIDTH)),
            _rope_table_spec(bm, seq // bm),
            _rope_table_spec(bm, seq // bm),
        ],
        out_specs=[
            pl.BlockSpec((bm, 2 * KV_WIDTH), _done_tile),
            pl.BlockSpec((bm, 2 * KV_WIDTH), _done_tile),
        ],
        out_shape=[
            jax.ShapeDtypeStruct((t, 2 * KV_WIDTH), BF16),
            jax.ShapeDtypeStruct((t, 2 * KV_WIDTH), BF16),
        ],
        scratch_shapes=[pltpu.VMEM((D_MODEL, 2 * KV_WIDTH), BF16)] + _acc_scratch(bm, 2 * KV_WIDTH),
        compiler_params=_params("arbitrary"),
        name="proj_kv",
    )(x_bf, w_in, cos_t, sin_t)


def _proj_gate_kernel(n_tiles, m_tiles, n_side, x_ref, w0_ref, w1_ref, b_ref, *refs):
    side_in, o_ref, side_out = refs[:n_side], refs[n_side], refs[n_side + 1:2 * n_side + 1]
    wbf_ref, acc0_ref, acc1_ref = refs[2 * n_side + 1:]
    step = pl.program_id(0)
    half = w0_ref.shape[1]

    @pl.when((step % m_tiles == 0) & (step < n_tiles))
    def _():
        wbf_ref[:, :half] = w0_ref[...].astype(BF16)
        wbf_ref[:, half:] = w1_ref[...].astype(BF16)

    def compute():
        for src, dst in zip(side_in, side_out):
            dst[...] = src[...].astype(BF16)
        return _dot(x_ref[...], wbf_ref[...])

    def epilogue(acc_ref, rows):
        o_ref[rows, :] = jax.nn.sigmoid(acc_ref[rows, :] + b_ref[...])

    _deferred_epilogue(step, n_tiles, (acc0_ref, acc1_ref), compute, epilogue)


def _proj_gate(x_bf, w_in, layer, b_gate, side_weights, bm, bn):
    t = x_bf.shape[0]
    m_tiles = t // bm
    n_tiles = m_tiles * (GATE_WIDTH // bn)
    half = bn // 2
    first = OFF_GATE // half
    cur = lambda s: jnp.minimum(s, n_tiles - 1)
    done = lambda s: jnp.maximum(s - 1, 0)
    slab = lambda w: (w.shape[1] // n_tiles, w.shape[2])
    return pl.pallas_call(
        functools.partial(_proj_gate_kernel, n_tiles, m_tiles, len(side_weights)),
        grid=(n_tiles + 1,),
        in_specs=[
            pl.BlockSpec((bm, D_MODEL), lambda s: (cur(s) % m_tiles, 0)),
            pl.BlockSpec((None, D_MODEL, half), lambda s: (layer, 0, first + 2 * (cur(s) // m_tiles))),
            pl.BlockSpec((None, D_MODEL, half), lambda s: (layer, 0, first + 2 * (cur(s) // m_tiles) + 1)),
            pl.BlockSpec((1, bn), lambda s: (0, done(s) // m_tiles)),
        ] + [pl.BlockSpec((None,) + slab(w), lambda s: (layer, cur(s), 0)) for w in side_weights],
        out_specs=[pl.BlockSpec((bm, bn), lambda s: (done(s) % m_tiles, done(s) // m_tiles))]
        + [pl.BlockSpec(slab(w), lambda s: (cur(s), 0)) for w in side_weights],
        out_shape=[jax.ShapeDtypeStruct((t, GATE_WIDTH), F32)]
        + [jax.ShapeDtypeStruct(w.shape[1:], BF16) for w in side_weights],
        scratch_shapes=[pltpu.VMEM((D_MODEL, bn), BF16)] + _acc_scratch(bm, bn),
        compiler_params=_params("arbitrary"),
        name="proj_gate",
    )(x_bf, w_in, w_in, b_gate, *side_weights)


def _swa_kernel(sink_ref, q_ref, kp_ref, kc_ref, vp_ref, vc_ref, o_ref):
    n = pl.program_id(1)
    pair_rows = 2 * WINDOW
    lane = lax.broadcasted_iota(jnp.int32, (WINDOW, LANES), 1)
    low_half = lane < HEAD_DIM
    row = lax.broadcasted_iota(jnp.int32, (pair_rows, WINDOW), 0)
    key = lax.broadcasted_iota(jnp.int32, (pair_rows, WINDOW), 1)
    in_own_block = key <= (row % WINDOW)
    top_rows = lax.broadcasted_iota(jnp.int32, (pair_rows, 1), 0) < WINDOW
    first_key = jnp.where(n > 0, 0, WINDOW)
    first_bias = jnp.where(lax.broadcasted_iota(jnp.int32, (1, WINDOW), 1) >= first_key, 0.0, -jnp.inf)
    for blk in range(SWA_BLOCKS):
        own = slice(blk * WINDOW, (blk + 1) * WINDOW)
        outs = []
        for p in range(N_Q_HEADS // 2):
            g = (2 * p) // (N_Q_HEADS // N_KV_HEADS)
            kv_cols = slice(g * LANES, (g + 1) * LANES)
            if blk == 0:
                k_prev, v_prev = kp_ref[:, kv_cols], vp_ref[:, kv_cols]
            else:
                before = slice((blk - 1) * WINDOW, blk * WINDOW)
                k_prev, v_prev = kc_ref[before, kv_cols], vc_ref[before, kv_cols]
            qs = jnp.concatenate([q_ref[own, 2 * p * LANES:(2 * p + 1) * LANES],
                                  q_ref[own, (2 * p + 1) * LANES:(2 * p + 2) * LANES]], axis=0)
            s_prev = _dot_nt(qs, k_prev)
            if blk == 0:
                s_prev = s_prev + first_bias
            s = jnp.where(in_own_block, _dot_nt(qs, kc_ref[own, kv_cols]), s_prev)
            sink = jnp.where(top_rows, sink_ref[2 * p], sink_ref[2 * p + 1])
            m = jnp.maximum(jnp.max(s, axis=-1, keepdims=True), sink)
            e = jnp.exp(s - m)
            denom = jnp.sum(e, axis=-1, keepdims=True) + jnp.exp(sink - m)
            e_own = jnp.where(in_own_block, e, 0.0).astype(BF16)
            e_prev = jnp.where(in_own_block, 0.0, e).astype(BF16)
            o2 = (_dot(e_own, vc_ref[own, kv_cols]) + _dot(e_prev, v_prev)) / denom
            outs.append(jnp.where(low_half, o2[:WINDOW], o2[WINDOW:]))
        o_ref[own, :] = jnp.concatenate(outs, axis=1).astype(BF16)


def _swa(q, k2, v2, sinks, bsz, seq):
    steps_per_seq = seq // (SWA_BLOCKS * WINDOW)
    cur = lambda b, n: (b * steps_per_seq + n, 0)
    prev = lambda b, n: (SWA_BLOCKS * (b * steps_per_seq + n) - jnp.minimum(n, 1), 0)
    return pl.pallas_call(
        _swa_kernel,
        grid=(bsz, steps_per_seq),
        in_specs=[
            pl.BlockSpec(memory_space=pltpu.SMEM),
            pl.BlockSpec((SWA_BLOCKS * WINDOW, 2 * ATTN_WIDTH), cur),
            pl.BlockSpec((WINDOW, 2 * KV_WIDTH), prev),
            pl.BlockSpec((SWA_BLOCKS * WINDOW, 2 * KV_WIDTH), cur),
            pl.BlockSpec((WINDOW, 2 * KV_WIDTH), prev),
            pl.BlockSpec((SWA_BLOCKS * WINDOW, 2 * KV_WIDTH), cur),
        ],
        out_specs=pl.BlockSpec((SWA_BLOCKS * WINDOW, ATTN_WIDTH), cur),
        out_shape=jax.ShapeDtypeStruct((bsz * seq, ATTN_WIDTH), BF16),
        compiler_params=_params("parallel", "parallel"),
        name="swa",
    )(sinks, q, k2, k2, v2, v2)


def _branch_kernel(gu_ref, vn_ref, at_ref, ga_ref, gb_ref, ws_ref, bs_ref, wa_ref, wb_ref, o_ref, gated_ref):
    bm = gu_ref.shape[0]
    t_idx = lax.broadcasted_iota(jnp.int32, (CHUNK, CHUNK), 0)
    s_idx = lax.broadcasted_iota(jnp.int32, (CHUNK, CHUNK), 1)
    causal = s_idx <= t_idx
    bs = bs_ref[...]
    for g in range(GMLP_GROUPS):
        w = jnp.where(causal, ws_ref[g], 0.0).astype(BF16)
        bias = bs[:, g:g + 1]
        cols = slice(g * LANES, (g + 1) * LANES)
        for c in range(bm // CHUNK):
            rows = slice(c * CHUNK, (c + 1) * CHUNK)
            mixed = _dot(w, vn_ref[rows, cols]) + bias
            gated_ref[rows, cols] = (gu_ref[rows, cols] * mixed).astype(BF16)
    ya = _dot(gated_ref[...], wa_ref[...])
    yb = _dot(at_ref[...], wb_ref[...])
    o_ref[...] = (ga_ref[...] * ya + gb_ref[...] * yb).astype(BF16)


def _branch(gu, vn, attn, gates, w_s, b_s_t, w_a, w_b, bm):
    t = gu.shape[0]
    row = lambda i: (i, 0)
    const2 = lambda i: (0, 0)
    return pl.pallas_call(
        _branch_kernel,
        grid=(t // bm,),
        in_specs=[
            pl.BlockSpec((bm, GMLP_WIDTH), row),
            pl.BlockSpec((bm, GMLP_WIDTH), row),
            pl.BlockSpec((bm, ATTN_WIDTH), row),
            pl.BlockSpec((bm, D_MODEL), lambda i: (i, 0)),
            pl.BlockSpec((bm, D_MODEL), lambda i: (i, 1)),
            pl.BlockSpec((GMLP_GROUPS, CHUNK, CHUNK), lambda i: (0, 0, 0)),
            pl.BlockSpec((CHUNK, GMLP_GROUPS), const2),
            pl.BlockSpec((GMLP_WIDTH, D_MODEL), const2),
            pl.BlockSpec((ATTN_WIDTH, D_MODEL), const2),
        ],
        out_specs=pl.BlockSpec((bm, D_MODEL), row),
        out_shape=jax.ShapeDtypeStruct((t, D_MODEL), BF16),
        scratch_shapes=[pltpu.VMEM((bm, GMLP_WIDTH), BF16)],
        compiler_params=_params("parallel"),
        name="branch",
    )(gu, vn, attn, gates, gates, w_s, b_s_t, w_a, w_b)


def _post_norm(g_ref, b_ref, o_ref, obf_ref):
    def epilogue(z_ref, rows):
        y = _layer_norm(z_ref[rows, :], g_ref[...], b_ref[...])
        o_ref[rows, :] = y
        obf_ref[rows, :] = y.astype(BF16)
    return epilogue


def _mem_kv_kernel(m_ref, w_ref, o_ref):
    o_ref[...] = _dot(m_ref[...].astype(BF16), w_ref[...].astype(BF16)).astype(BF16)


def _mem_kv(mem2d, w_xkv, layer):
    rows = mem2d.shape[0]
    return pl.pallas_call(
        _mem_kv_kernel,
        grid=(2,),
        in_specs=[
            pl.BlockSpec((rows, D_MODEL), lambda j: (0, 0)),
            pl.BlockSpec((None, D_MODEL, X_WIDTH), lambda j: (layer, 0, j)),
        ],
        out_specs=pl.BlockSpec((rows, X_WIDTH), lambda j: (0, j)),
        out_shape=jax.ShapeDtypeStruct((rows, 2 * X_WIDTH), BF16),
        compiler_params=_params("parallel"),
        name="mem_kv",
    )(mem2d, w_xkv)


def _post_mixer_kernel(n_tiles, m_ref, x_ref, k_ref, v_ref, wo_ref, g1_ref, b1_ref, wq_ref, wxo_ref, g2_ref, b2_ref,
                       o_ref, obf_ref, z0_ref, z1_ref):
    def compute_z():
        z1 = ALPHA * x_ref[...] + _dot(m_ref[...], wo_ref[...])
        x1 = _layer_norm(z1, g1_ref[...], b1_ref[...])
        q = _dot(x1.astype(BF16), wq_ref[...]).astype(BF16)
        scale = X_HEAD_DIM ** -0.5
        outs = []
        for h in range(X_HEADS):
            cols = slice(h * X_HEAD_DIM, (h + 1) * X_HEAD_DIM)
            s = _dot_nt(q[:, cols], k_ref[:, cols]) * scale
            m = jnp.max(s, axis=-1, keepdims=True)
            p = jnp.exp(s - m)
            denom = jnp.sum(p, axis=-1, keepdims=True)
            outs.append(_dot(p.astype(BF16), v_ref[:, cols]) / denom)
        o = jnp.concatenate(outs, axis=1).astype(BF16)
        return ALPHA * x1 + _dot(o, wxo_ref[...])

    _deferred_epilogue(pl.program_id(0), n_tiles, (z0_ref, z1_ref), compute_z,
                       _post_norm(g2_ref, b2_ref, o_ref, obf_ref))


def _post_mixer(merged, x, kv, w_o, ln1_g, ln1_b, w_xq, w_xo, ln2_g, ln2_b, bm, seq):
    t = x.shape[0]
    n_tiles = t // bm
    tiles_per_seq = seq // bm
    batch = lambda s: jnp.minimum(s, n_tiles - 1) // tiles_per_seq
    const2 = lambda s: (0, 0)
    return pl.pallas_call(
        functools.partial(_post_mixer_kernel, n_tiles),
        grid=(n_tiles + 1,),
        in_specs=[
            pl.BlockSpec((bm, D_MODEL), _cur_tile(n_tiles)),
            pl.BlockSpec((bm, D_MODEL), _cur_tile(n_tiles)),
            pl.BlockSpec((MEM_LEN, X_WIDTH), lambda s: (batch(s), 0)),
            pl.BlockSpec((MEM_LEN, X_WIDTH), lambda s: (batch(s), 1)),
            pl.BlockSpec((D_MODEL, D_MODEL), const2),
            pl.BlockSpec((1, D_MODEL), const2),
            pl.BlockSpec((1, D_MODEL), const2),
            pl.BlockSpec((D_MODEL, X_WIDTH), const2),
            pl.BlockSpec((X_WIDTH, D_MODEL), const2),
            pl.BlockSpec((1, D_MODEL), const2),
            pl.BlockSpec((1, D_MODEL), const2),
        ],
        out_specs=[pl.BlockSpec((bm, D_MODEL), _done_tile), pl.BlockSpec((bm, D_MODEL), _done_tile)],
        out_shape=[jax.ShapeDtypeStruct((t, D_MODEL), F32), jax.ShapeDtypeStruct((t, D_MODEL), BF16)],
        scratch_shapes=_acc_scratch(bm, D_MODEL),
        compiler_params=_params("arbitrary"),
        name="post_mixer",
    )(merged, x, kv, kv, w_o, ln1_g, ln1_b, w_xq, w_xo, ln2_g, ln2_b)


def _mlp_kernel(xbf_ref, x_ref, wu_ref, wd_ref, g_ref, b_ref, o_ref, obf_ref, acc_ref):
    f = pl.program_id(1)

    def contribution():
        h = jnp.maximum(_dot(xbf_ref[...], wu_ref[...]), 0.0)
        return _dot((h * h).astype(BF16), wd_ref[...])

    @pl.when(f == 0)
    def _():
        acc_ref[...] = ALPHA * x_ref[...] + contribution()

    @pl.when(f > 0)
    def _():
        acc_ref[...] += contribution()

    @pl.when(f == pl.num_programs(1) - 1)
    def _():
        y = _layer_norm(acc_ref[...], g_ref[...], b_ref[...])
        o_ref[...] = y
        obf_ref[...] = y.astype(BF16)


def _mlp(x_bf, x, w_up, w_down, ln_g, ln_b, bm, bf):
    t = x.shape[0]
    n_f = D_FF // bf
    row = lambda i, f: (i, 0)
    const2 = lambda i, f: (0, 0)
    return pl.pallas_call(
        _mlp_kernel,
        grid=(t // bm, n_f),
        in_specs=[
            pl.BlockSpec((bm, D_MODEL), row),
            pl.BlockSpec((bm, D_MODEL), row),
            pl.BlockSpec((D_MODEL, bf), lambda i, f: (0, f)),
            pl.BlockSpec((bf, D_MODEL), lambda i, f: (f, 0)),
            pl.BlockSpec((1, D_MODEL), const2),
            pl.BlockSpec((1, D_MODEL), const2),
        ],
        out_specs=[pl.BlockSpec((bm, D_MODEL), row), pl.BlockSpec((bm, D_MODEL), row)],
        out_shape=[jax.ShapeDtypeStruct((t, D_MODEL), F32), jax.ShapeDtypeStruct((t, D_MODEL), BF16)],
        scratch_shapes=[pltpu.VMEM((bm, D_MODEL), F32)],
        compiler_params=_params("parallel", "arbitrary"),
        name="mlp",
    )(x_bf, x, w_up, w_down, ln_g, ln_b)


def _rope_tables(seq):
    inv = 1.0 / (ROPE_THETA ** (jnp.arange(0, HEAD_DIM, 2, dtype=F32) / HEAD_DIM))
    ang = jnp.arange(seq, dtype=F32)[:, None] * inv[None, :]
    cos, sin = jnp.cos(ang), jnp.sin(ang)
    reps = LANES // HEAD_DIM
    cos_t = jnp.tile(jnp.concatenate([cos, cos], axis=1), (1, reps))
    sin_t = jnp.tile(jnp.concatenate([-sin, sin], axis=1), (1, reps))
    return cos_t, sin_t


def kernel(x, mem, w_in, b_gate, ln_v_g, ln_v_b, w_s, b_s, sinks, w_br_a, w_br_b, w_o, ln1_g, ln1_b,
           w_xq, w_xkv, w_xo, ln2_g, ln2_b, w_up, w_down, ln3_g, ln3_b):
    bsz, seq, d = x.shape
    t = bsz * seq
    cos_t, sin_t = _rope_tables(seq)
    xf = x.reshape(t, d)
    mem2d = mem.reshape(bsz * MEM_LEN, d)
    row = lambda a: a.reshape(1, -1)
    for l in range(DEPTH):
        if l == 0:
            gu, xb = _proj_u_from_f32(xf, w_in, l, bm=512)
        else:
            gu = _proj_u(xb, w_in, l, bm=1024)
        vn = _proj_v(xb, w_in, l, row(ln_v_g[l]), row(ln_v_b[l]), bm=1024)
        q = _proj_q(xb, w_in, l, cos_t, sin_t, bm=1024, seq=seq)
        k2, v2 = _proj_kv(xb, w_in, l, cos_t, sin_t, bm=1024, seq=seq)
        gates, w_a, w_b, w_o_bf, w_up_bf, w_down_bf = _proj_gate(
            xb, w_in, l, row(b_gate[l]), [w_br_a, w_br_b, w_o, w_up, w_down], bm=1024, bn=1024)
        attn = _swa(q, k2, v2, sinks[l], bsz, seq)
        merged = _branch(gu, vn, attn, gates, w_s[l], b_s[l].T, w_a, w_b, bm=512)
        kv = _mem_kv(mem2d, w_xkv, l)
        xf, xb = _post_mixer(merged, xf, kv, w_o_bf, row(ln1_g[l]), row(ln1_b[l]), w_xq[l].astype(BF16),
                             w_xo[l].astype(BF16), row(ln2_g[l]), row(ln2_b[l]), bm=512, seq=seq)
        xf, xb = _mlp(xb, xf, w_up_bf, w_down_bf, row(ln3_g[l]), row(ln3_b[l]), bm=512, bf=1024)
    return xf.reshape(bsz, seq, d)
```
